```python
import math
import jax, jax.numpy as jnp
from jax import lax
import numpy as np

D_MODEL = 1024
BATCH = 4
SEQ = 4096
DEPTH = 1

N_SB_HEADS = 8
SB_HEAD_DIM = 64
SB_WIDTH = N_SB_HEADS * SB_HEAD_DIM
Q_BLOCK = 128
N_SG_GROUPS = 8
SG_GROUP_DIM = 64
SG_WIDTH = N_SG_GROUPS * SG_GROUP_DIM
CHUNK = 128
N_BRANCH = 2
COL_Q = 0
COL_K = SB_WIDTH
COL_V = 2 * SB_WIDTH
COL_U = 3 * SB_WIDTH
COL_VSG = 3 * SB_WIDTH + SG_WIDTH
COL_GATE = 3 * SB_WIDTH + 2 * SG_WIDTH
PROJ_COLS = COL_GATE + N_BRANCH * D_MODEL
D_FF = 2816
CONV_WIDTH = 3
LN_EPS = 1e-5
DN_ALPHA = (2 * DEPTH) ** 0.25
DN_BETA = (8 * DEPTH) ** -0.25

kernel_name = "hybrid_stickbreak_sgu_convffn_deepnorm"


def layer_norm(x, g, b):
    xf = x.astype(jnp.float32)
    mu = jnp.mean(xf, axis=-1, keepdims=True)
    var = jnp.mean(jnp.square(xf - mu), axis=-1, keepdims=True)
    return ((xf - mu) * lax.rsqrt(var + LN_EPS) * g + b).astype(x.dtype)


def stick_breaking_block(q_blk, k_pre, v_pre, q_start):
    qb = q_blk.shape[2]
    L = k_pre.shape[2]
    z = jnp.einsum('bhqd,bhkd->bhqk', q_blk, k_pre,
                   preferred_element_type=jnp.float32) / math.sqrt(SB_HEAD_DIM)
    q_pos = q_start + jnp.arange(qb)[:, None]
    k_pos = jnp.arange(L)[None, :]
    causal = k_pos < q_pos
    log_beta = jax.nn.log_sigmoid(z)
    log_1m = jnp.where(causal, jax.nn.log_sigmoid(-z), 0.0)
    tail = lax.cumsum(log_1m, axis=3, reverse=True) - log_1m
    w = jnp.where(causal, jnp.exp(log_beta + tail), 0.0)
    return jnp.einsum('bhqk,bhkd->bhqd', w.astype(v_pre.dtype), v_pre)


def stick_breaking_attention(q, k, v):
    S = q.shape[2]
    outs = []
    for i in range(S // Q_BLOCK):
        qs = i * Q_BLOCK
        end = qs + Q_BLOCK
        outs.append(stick_breaking_block(q[:, :, qs:end], k[:, :, :end], v[:, :, :end], qs))
    return jnp.concatenate(outs, axis=2)


def spatial_gating(u, v, ln_g, ln_b, w_s, b_s):
    B, S, _ = u.shape
    v = layer_norm(v, ln_g, ln_b)
    vc = v.reshape(B, S // CHUNK, CHUNK, N_SG_GROUPS, SG_GROUP_DIM)
    w_causal = w_s * jnp.tril(jnp.ones((CHUNK, CHUNK), dtype=w_s.dtype))
    mixed = jnp.einsum('gts,bnsgc->bntgc', w_causal, vc) + b_s.T[None, None, :, :, None]
    return u * mixed.reshape(B, S, SG_WIDTH)


def causal_dwconv(h, w, b):
    S = h.shape[1]
    hp = jnp.pad(h, ((0, 0), (CONV_WIDTH - 1, 0), (0, 0)))
    out = b
    for i in range(CONV_WIDTH):
        out = out + hp[:, i:i + S] * w[i]
    return out


def mixer_sublayer(h, w_in, b_gate, ln_sg_g, ln_sg_b, w_spatial, b_spatial,
                   w_branch_a, w_branch_b, w_out, b_out, ln1_g, ln1_b):
    B, S, _ = h.shape
    proj = h @ w_in
    q, k, v, u_sg, v_sg, gates = jnp.split(
        proj, [COL_K, COL_V, COL_U, COL_VSG, COL_GATE], axis=-1)

    def heads(t):
        return t.reshape(B, S, N_SB_HEADS, SB_HEAD_DIM).transpose(0, 2, 1, 3)

    y_a = stick_breaking_attention(heads(q), heads(k), heads(v))
    y_a = y_a.transpose(0, 2, 1, 3).reshape(B, S, SB_WIDTH)
    y_b = spatial_gating(jax.nn.gelu(u_sg), jax.nn.gelu(v_sg),
                         ln_sg_g, ln_sg_b, w_spatial, b_spatial)
    g = jax.nn.sigmoid(gates + b_gate)
    g_a, g_b = jnp.split(g, N_BRANCH, axis=-1)
    merged = g_a * (y_a @ w_branch_a) + g_b * (y_b @ w_branch_b)
    y = merged @ w_out + b_out
    return layer_norm(DN_ALPHA * h + y, ln1_g, ln1_b)


def ffn_sublayer(h, w_up, conv_w, conv_b, w_down, b_down, ln2_g, ln2_b):
    up = causal_dwconv(h @ w_up, conv_w, conv_b)
    a, bv = jnp.split(up, 2, axis=-1)
    y = (jax.nn.silu(a) * bv) @ w_down + b_down
    return layer_norm(DN_ALPHA * h + y, ln2_g, ln2_b)


def setup_inputs(seed: int = 0) -> dict:
    key = jax.random.key(seed)
    ks = jax.random.split(key, 24)
    f32 = jnp.float32

    def nrm(k, shape, scale):
        return jax.random.normal(k, shape, dtype=f32) * scale

    x = nrm(ks[0], (BATCH, SEQ, D_MODEL), 1.0)
    col_scale = jnp.ones((PROJ_COLS,), f32).at[COL_V:COL_U].set(DN_BETA)
    w_in = nrm(ks[1], (DEPTH, D_MODEL, PROJ_COLS), D_MODEL ** -0.5) * col_scale
    b_gate = nrm(ks[2], (DEPTH, N_BRANCH * D_MODEL), 0.02)
    ln_sg_g = 1.0 + nrm(ks[3], (DEPTH, SG_WIDTH), 0.02)
    ln_sg_b = nrm(ks[4], (DEPTH, SG_WIDTH), 0.02)
    w_spatial = nrm(ks[5], (DEPTH, N_SG_GROUPS, CHUNK, CHUNK), 0.5 * CHUNK ** -0.5)
    b_spatial = 1.0 + nrm(ks[6], (DEPTH, N_SG_GROUPS, CHUNK), 0.02)
    w_branch_a = nrm(ks[7], (DEPTH, SB_WIDTH, D_MODEL), SB_WIDTH ** -0.5)
    w_branch_b = nrm(ks[8], (DEPTH, SG_WIDTH, D_MODEL), SG_WIDTH ** -0.5)
    w_out = nrm(ks[9], (DEPTH, D_MODEL, D_MODEL), DN_BETA * D_MODEL ** -0.5)
    b_out = nrm(ks[10], (DEPTH, D_MODEL), 0.02)
    ln1_g = 1.0 + nrm(ks[11], (DEPTH, D_MODEL), 0.02)
    ln1_b = nrm(ks[12], (DEPTH, D_MODEL), 0.02)
    w_up = nrm(ks[13], (DEPTH, D_MODEL, 2 * D_FF), DN_BETA * D_MODEL ** -0.5)
    conv_w = nrm(ks[14], (DEPTH, CONV_WIDTH, 2 * D_FF), CONV_WIDTH ** -0.5)
    conv_b = nrm(ks[15], (DEPTH, 2 * D_FF), 0.02)
    w_down = nrm(ks[16], (DEPTH, D_FF, D_MODEL), DN_BETA * D_FF ** -0.5)
    b_down = nrm(ks[17], (DEPTH, D_MODEL), 0.02)
    ln2_g = 1.0 + nrm(ks[18], (DEPTH, D_MODEL), 0.02)
    ln2_b = nrm(ks[19], (DEPTH, D_MODEL), 0.02)
    return {"x": x, "w_in": w_in, "b_gate": b_gate, "ln_sg_g": ln_sg_g, "ln_sg_b": ln_sg_b,
            "w_spatial": w_spatial, "b_spatial": b_spatial, "w_branch_a": w_branch_a,
            "w_branch_b": w_branch_b, "w_out": w_out, "b_out": b_out,
            "ln1_g": ln1_g, "ln1_b": ln1_b, "w_up": w_up, "conv_w": conv_w, "conv_b": conv_b,
            "w_down": w_down, "b_down": b_down, "ln2_g": ln2_g, "ln2_b": ln2_b}


def reference(x, w_in, b_gate, ln_sg_g, ln_sg_b, w_spatial, b_spatial, w_branch_a,
              w_branch_b, w_out, b_out, ln1_g, ln1_b, w_up, conv_w, conv_b,
              w_down, b_down, ln2_g, ln2_b):
    h = x
    for l in range(DEPTH):
        h = mixer_sublayer(h, w_in[l], b_gate[l], ln_sg_g[l], ln_sg_b[l], w_spatial[l],
                           b_spatial[l], w_branch_a[l], w_branch_b[l], w_out[l], b_out[l],
                           ln1_g[l], ln1_b[l])
        h = ffn_sublayer(h, w_up[l], conv_w[l], conv_b[l], w_down[l], b_down[l],
                         ln2_g[l], ln2_b[l])
    return h
```

```python
import functools

import jax
import jax.numpy as jnp
from jax import lax
from jax.experimental import pallas as pl
from jax.experimental.pallas import tpu as pltpu

F32 = jnp.float32
BF16 = jnp.bfloat16

D_MODEL = 1024
N_HEADS = 8
HEAD_DIM = 64
SB_WIDTH = N_HEADS * HEAD_DIM
SG_WIDTH = 512
N_SG_GROUPS = 8
CHUNK = 128
Q_BLOCK = 128
D_FF = 2816
LN_EPS = 1e-5
DN_ALPHA = 2.0 ** 0.25

LANES = 128
PAIR = 2 * HEAD_DIM
N_PAIRS = N_HEADS // 2
FF_CHUNK = 256
VMEM_LIMIT = 56 * 1024 * 1024


def _ln(x, g, b):
    mu = jnp.mean(x, axis=-1, keepdims=True)
    xc = x - mu
    var = jnp.mean(xc * xc, axis=-1, keepdims=True)
    return xc * lax.rsqrt(var + LN_EPS) * g + b


def _gelu_tanh(x):
    c = 0.7978845608028654
    return x * (0.5 * (1.0 + jnp.tanh(c * (x + 0.044715 * (x * x * x)))))


def _sigmoid(x):
    return 1.0 / (1.0 + jnp.exp(-x))


def _dot(a, b):
    return jnp.dot(a, b, preferred_element_type=F32)


def _resident(shape):
    nd = len(shape)
    return pl.BlockSpec(shape, lambda *_: (0,) * nd, pipeline_mode=pl.Buffered(1))


def _qkv_kernel(x_ref, wq_ref, wkt_ref, wv_ref, q_ref, kt_ref, v_ref):
    xb = x_ref[...].astype(BF16)
    q_ref[...] = (_dot(xb, wq_ref[...]) * 0.125).astype(BF16)
    v_ref[...] = _dot(xb, wv_ref[...]).astype(BF16)
    kt = lax.dot_general(wkt_ref[...], xb, (((1,), (1,)), ((), ())),
                         preferred_element_type=F32)
    kt_ref[0] = kt.astype(BF16)


def _qkv_proj(x2, wq, wkt, wv, batch, seq, tm):
    tokens = x2.shape[0]
    tiles_per_seq = seq // tm
    return pl.pallas_call(
        _qkv_kernel,
        grid=(tokens // tm,),
        in_specs=[
            pl.BlockSpec((tm, D_MODEL), lambda i: (i, 0)),
            _resident((D_MODEL, SB_WIDTH)),
            _resident((SB_WIDTH, D_MODEL)),
            _resident((D_MODEL, SB_WIDTH)),
        ],
        out_specs=[
            pl.BlockSpec((tm, SB_WIDTH), lambda i: (i, 0)),
            pl.BlockSpec((1, SB_WIDTH, tm), lambda i: (i // tiles_per_seq, 0, i % tiles_per_seq)),
            pl.BlockSpec((tm, SB_WIDTH), lambda i: (i, 0)),
        ],
        out_shape=[
            jax.ShapeDtypeStruct((tokens, SB_WIDTH), BF16),
            jax.ShapeDtypeStruct((batch, SB_WIDTH, seq), BF16),
            jax.ShapeDtypeStruct((tokens, SB_WIDTH), BF16),
        ],
        compiler_params=pltpu.CompilerParams(
            dimension_semantics=("arbitrary",), vmem_limit_bytes=VMEM_LIMIT),
        name="qkv_proj",
    )(x2, wq, wkt, wv)


def _softplus(z):
    return jnp.maximum(z, 0.0) + jnp.log(1.0 + jnp.exp(-jnp.abs(z)))


def _sb_attn_kernel(q_ref, kt_ref, v_ref, tri_ref, o_ref, kbd_ref, vbd_ref, carry_ref, acc_ref,
                    *, n_kblocks):
    qi = pl.program_id(2)

    @pl.when(qi == 0)
    def _():
        row_lo = lax.broadcasted_iota(jnp.int32, (PAIR, Q_BLOCK), 0) < HEAD_DIM
        lane_lo = lax.broadcasted_iota(jnp.int32, (Q_BLOCK, PAIR), 1) < HEAD_DIM
        zero = jnp.zeros((PAIR, Q_BLOCK), BF16)
        for j in range(n_kblocks):
            kt = kt_ref[0, :, j * Q_BLOCK:(j + 1) * Q_BLOCK]
            kbd_ref[j, :, 0:Q_BLOCK] = jnp.where(row_lo, kt, zero)
            kbd_ref[j, :, Q_BLOCK:2 * Q_BLOCK] = jnp.where(row_lo, zero, kt)
            vt = v_ref[0, j * Q_BLOCK:(j + 1) * Q_BLOCK, :]
            vbd_ref[j, 0:Q_BLOCK, :] = jnp.where(lane_lo, vt, zero)
            vbd_ref[j, Q_BLOCK:2 * Q_BLOCK, :] = jnp.where(lane_lo, zero, vt)

    q2 = q_ref[0]
    tri = tri_ref[...]

    def tile(j, causal):
        z = _dot(q2, kbd_ref[j])
        sp = _softplus(z)
        if causal is not None:
            sp = jnp.where(causal, sp, 0.0)
        hi = sp.astype(BF16)
        lo = (sp - hi.astype(F32)).astype(BF16)
        ws = []
        for h in range(2):
            sl = slice(h * Q_BLOCK, (h + 1) * Q_BLOCK)
            cr = _dot(jnp.concatenate([hi[:, sl], lo[:, sl]], axis=1), tri)
            e = z[:, sl] + cr[:, :Q_BLOCK] + carry_ref[:, sl]
            if causal is not None:
                e = jnp.where(causal[:, sl], e, -1e30)
            ws.append(jnp.exp(e))
            carry_ref[:, sl] = carry_ref[:, sl] + cr[:, Q_BLOCK:]
        w2 = jnp.concatenate(ws, axis=1).astype(BF16)
        acc_ref[...] += _dot(w2, vbd_ref[j])

    carry_ref[...] = jnp.zeros_like(carry_ref)
    acc_ref[...] = jnp.zeros_like(acc_ref)
    row = lax.broadcasted_iota(jnp.int32, (Q_BLOCK, 2 * Q_BLOCK), 0)
    col = lax.broadcasted_iota(jnp.int32, (Q_BLOCK, 2 * Q_BLOCK), 1) & (Q_BLOCK - 1)
    tile(qi, col < row)

    def body(i, c):
        tile(qi - 1 - i, None)
        return c

    lax.fori_loop(0, qi, body, 0)
    o_ref[0] = acc_ref[...].astype(BF16)


def _sb_attention(q3, kt3, v3, tri):
    batch, seq, _ = q3.shape
    n_blocks = seq // Q_BLOCK
    return pl.pallas_call(
        functools.partial(_sb_attn_kernel, n_kblocks=n_blocks),
        grid=(batch, N_PAIRS, n_blocks),
        in_specs=[
            pl.BlockSpec((1, Q_BLOCK, PAIR), lambda b, p, i: (b, i, p)),
            pl.BlockSpec((1, PAIR, seq), lambda b, p, i: (b, p, 0)),
            pl.BlockSpec((1, seq, PAIR), lambda b, p, i: (b, 0, p)),
            _resident((2 * Q_BLOCK, 2 * Q_BLOCK)),
        ],
        out_specs=pl.BlockSpec((1, Q_BLOCK, PAIR), lambda b, p, i: (b, i, p)),
        out_shape=jax.ShapeDtypeStruct((batch, seq, SB_WIDTH), BF16),
        scratch_shapes=[
            pltpu.VMEM((n_blocks, PAIR, 2 * Q_BLOCK), BF16),
            pltpu.VMEM((n_blocks, 2 * Q_BLOCK, PAIR), BF16),
            pltpu.VMEM((Q_BLOCK, 2 * Q_BLOCK), F32),
            pltpu.VMEM((Q_BLOCK, PAIR), F32),
        ],
        compiler_params=pltpu.CompilerParams(
            dimension_semantics=("arbitrary", "arbitrary", "arbitrary"),
            vmem_limit_bytes=VMEM_LIMIT),
        name="sb_attn",
    )(q3, kt3, v3, tri)


def _mixer_out_kernel(x_ref, ya_ref, wu_ref, wvs_ref, wg_ref, bg_ref, lsg_ref, lsb_ref, wsp_ref,
                      bsp_ref, wa_ref, wb_ref, wo_ref, bo_ref, g1_ref, b1_ref, o_ref):
    x = x_ref[...]
    xb = x.astype(BF16)
    tm = x.shape[0]

    u = _gelu_tanh(_dot(xb, wu_ref[...]))
    vs = _ln(_gelu_tanh(_dot(xb, wvs_ref[...])), lsg_ref[...], lsb_ref[...]).astype(BF16)

    tril = (lax.broadcasted_iota(jnp.int32, (CHUNK, CHUNK), 1)
            <= lax.broadcasted_iota(jnp.int32, (CHUNK, CHUNK), 0))
    wsp = [jnp.where(tril, wsp_ref[g], 0.0).astype(BF16) for g in range(N_SG_GROUPS)]
    lane_lo = lax.broadcasted_iota(jnp.int32, (CHUNK, LANES), 1) < HEAD_DIM
    zero = jnp.zeros((CHUNK, LANES), BF16)
    rows = []
    for c in range(tm // CHUNK):
        parts = []
        for p in range(N_SG_GROUPS // 2):
            v2 = vs[c * CHUNK:(c + 1) * CHUNK, p * LANES:(p + 1) * LANES]
            parts.append(_dot(wsp[2 * p], jnp.where(lane_lo, v2, zero))
                         + _dot(wsp[2 * p + 1], jnp.where(lane_lo, zero, v2)))
        rows.append(jnp.concatenate(parts, axis=1) + bsp_ref[...])
    yb = (u * jnp.concatenate(rows, axis=0)).astype(BF16)

    g = _sigmoid(_dot(xb, wg_ref[...]) + bg_ref[...])
    merged = (g[:, :D_MODEL] * _dot(ya_ref[...], wa_ref[...])
              + g[:, D_MODEL:] * _dot(yb, wb_ref[...]))
    y = _dot(merged.astype(BF16), wo_ref[...]) + bo_ref[...]
    o_ref[...] = _ln(DN_ALPHA * x + y, g1_ref[...], b1_ref[...])


def _mixer_out(x2, ya2, wu, wvs, wg, bg, lsg, lsb, wsp, bsp, wa, wb, wo, bo, g1, b1, tm):
    tokens = x2.shape[0]
    return pl.pallas_call(
        _mixer_out_kernel,
        grid=(tokens // tm,),
        in_specs=[
            pl.BlockSpec((tm, D_MODEL), lambda i: (i, 0)),
            pl.BlockSpec((tm, SB_WIDTH), lambda i: (i, 0)),
            _resident(wu.shape), _resident(wvs.shape), _resident(wg.shape), _resident(bg.shape),
            _resident(lsg.shape), _resident(lsb.shape), _resident(wsp.shape), _resident(bsp.shape),
            _resident(wa.shape), _resident(wb.shape), _resident(wo.shape), _resident(bo.shape),
            _resident(g1.shape), _resident(b1.shape),
        ],
        out_specs=pl.BlockSpec((tm, D_MODEL), lambda i: (i, 0)),
        out_shape=jax.ShapeDtypeStruct((tokens, D_MODEL), F32),
        compiler_params=pltpu.CompilerParams(
            dimension_semantics=("arbitrary",), vmem_limit_bytes=VMEM_LIMIT),
        name="mixer_out",
    )(x2, ya2, wu, wvs, wg, bg, lsg, lsb, wsp, bsp, wa, wb, wo, bo, g1, b1)


def _conv_ffn_kernel(h_ref, wup_ref, cw_ref, cb_ref, wd_ref, bd_ref, g2_ref, b2_ref, o_ref, tail_ref,
                     *, tiles_per_seq):
    h = h_ref[...]
    hb = h.astype(BF16)
    tm = h.shape[0]
    seq_start = (pl.program_id(0) % tiles_per_seq) == 0
    row8 = lax.broadcasted_iota(jnp.int32, (8, FF_CHUNK), 0)

    def conv_cols(c0):
        sl = slice(c0, c0 + FF_CHUNK)
        up = _dot(hb, wup_ref[:, sl])
        prev = jnp.where(seq_start, 0.0, tail_ref[:, sl])
        tail_ref[:, sl] = up[tm - 8:tm]
        s1 = pltpu.roll(up, 1, axis=0)
        s2 = pltpu.roll(up, 2, axis=0)
        head = up[0:8]
        p1 = pltpu.roll(prev, 1, axis=0)
        p2 = pltpu.roll(prev, 2, axis=0)
        h1 = jnp.where(row8 < 1, p1, pltpu.roll(head, 1, axis=0))
        h2 = jnp.where(row8 < 2, p2, pltpu.roll(head, 2, axis=0))
        s1 = jnp.concatenate([h1, s1[8:]], axis=0)
        s2 = jnp.concatenate([h2, s2[8:]], axis=0)
        cw = cw_ref[:, sl]
        return cb_ref[:, sl] + cw[0:1] * s2 + cw[1:2] * s1 + cw[2:3] * up

    acc = jnp.zeros((tm, D_MODEL), F32)
    for c in range(D_FF // FF_CHUNK):
        a = conv_cols(c * FF_CHUNK)
        b = conv_cols(D_FF + c * FF_CHUNK)
        act = (a * _sigmoid(a) * b).astype(BF16)
        acc = acc + _dot(act, wd_ref[c * FF_CHUNK:(c + 1) * FF_CHUNK, :])
    y = acc + bd_ref[...]
    o_ref[...] = _ln(DN_ALPHA * h + y, g2_ref[...], b2_ref[...])


def _conv_ffn(h2, wup, cw, cb, wd, bd, g2, b2, seq, tm):
    tokens = h2.shape[0]
    return pl.pallas_call(
        functools.partial(_conv_ffn_kernel, tiles_per_seq=seq // tm),
        grid=(tokens // tm,),
        in_specs=[
            pl.BlockSpec((tm, D_MODEL), lambda i: (i, 0)),
            _resident(wup.shape), _resident(cw.shape), _resident(cb.shape), _resident(wd.shape),
            _resident(bd.shape), _resident(g2.shape), _resident(b2.shape),
        ],
        out_specs=pl.BlockSpec((tm, D_MODEL), lambda i: (i, 0)),
        out_shape=jax.ShapeDtypeStruct((tokens, D_MODEL), F32),
        scratch_shapes=[pltpu.VMEM((8, 2 * D_FF), F32)],
        compiler_params=pltpu.CompilerParams(
            dimension_semantics=("arbitrary",), vmem_limit_bytes=VMEM_LIMIT),
        name="conv_ffn",
    )(h2, wup, cw, cb, wd, bd, g2, b2)


def _tri_const():
    j = jnp.arange(Q_BLOCK)[:, None]
    s = jnp.arange(Q_BLOCK)[None, :]
    half = jnp.concatenate([-(j >= s).astype(F32), -jnp.ones((Q_BLOCK, Q_BLOCK), F32)], axis=1)
    return jnp.concatenate([half, half], axis=0).astype(BF16)


def _layer(h, w_in, b_gate, ln_sg_g, ln_sg_b, w_spatial, b_spatial, w_branch_a, w_branch_b, w_out,
           b_out, ln1_g, ln1_b, w_up, conv_w, conv_b, w_down, b_down, ln2_g, ln2_b):
    batch, seq, _ = h.shape
    tokens = batch * seq
    x2 = h.reshape(tokens, D_MODEL)
    row = lambda v: v.reshape(1, -1)

    wq = w_in[:, 0:SB_WIDTH].astype(BF16)
    wkt = w_in[:, SB_WIDTH:2 * SB_WIDTH].T.astype(BF16)
    wv = w_in[:, 2 * SB_WIDTH:3 * SB_WIDTH].astype(BF16)
    wu = w_in[:, 3 * SB_WIDTH:3 * SB_WIDTH + SG_WIDTH].astype(BF16)
    wvs = w_in[:, 3 * SB_WIDTH + SG_WIDTH:3 * SB_WIDTH + 2 * SG_WIDTH].astype(BF16)
    wg = w_in[:, 3 * SB_WIDTH + 2 * SG_WIDTH:].astype(BF16)
    bsp = jnp.repeat(b_spatial.T, SG_WIDTH // N_SG_GROUPS, axis=1)

    q2, kt3, v2 = _qkv_proj(x2, wq, wkt, wv, batch, seq, tm=512)
    ya3 = _sb_attention(q2.reshape(batch, seq, SB_WIDTH), kt3, v2.reshape(batch, seq, SB_WIDTH),
                        _tri_const())
    h1 = _mixer_out(x2, ya3.reshape(tokens, SB_WIDTH), wu, wvs, wg, row(b_gate), row(ln_sg_g),
                    row(ln_sg_b), w_spatial, bsp, w_branch_a.astype(BF16), w_branch_b.astype(BF16),
                    w_out.astype(BF16), row(b_out), row(ln1_g), row(ln1_b), tm=512)
    h2 = _conv_ffn(h1, w_up.astype(BF16), conv_w, row(conv_b), w_down.astype(BF16), row(b_down),
                   row(ln2_g), row(ln2_b), seq, tm=512)
    return h2.reshape(batch, seq, D_MODEL)


def kernel(x, w_in, b_gate, ln_sg_g, ln_sg_b, w_spatial, b_spatial, w_branch_a, w_branch_b, w_out,
           b_out, ln1_g, ln1_b, w_up, conv_w, conv_b, w_down, b_down, ln2_g, ln2_b):
    h = x
    for l in range(w_in.shape[0]):
        h = _layer(h, w_in[l], b_gate[l], ln_sg_g[l], ln_sg_b[l], w_spatial[l], b_spatial[l],
                   w_branch_a[l], w_branch_b[l], w_out[l], b_out[l], ln1_g[l], ln1_b[l], w_up[l],
                   conv_w[l], conv_b[l], w_down[l], b_down[l], ln2_g[l], ln2_b[l])
    return h
```

```python
import functools

import jax
import jax.numpy as jnp
from jax import lax
from jax.experimental import pallas as pl
from jax.experimental.pallas import tpu as pltpu

F32 = jnp.float32
BF16 = jnp.bfloat16

D_MODEL = 1024
N_HEADS = 8
HEAD_DIM = 64
SB_WIDTH = N_HEADS * HEAD_DIM
SG_WIDTH = 512
N_SG_GROUPS = 8
CHUNK = 128
Q_BLOCK = 128
D_FF = 2816
LN_EPS = 1e-5
DN_ALPHA = 2.0 ** 0.25

LANES = 128
PAIR = 2 * HEAD_DIM
N_PAIRS = N_HEADS // 2
FF_CHUNK = 256
VMEM_LIMIT = 56 * 1024 * 1024


def _ln(x, g, b):
    mu = jnp.mean(x, axis=-1, keepdims=True)
    xc = x - mu
    var = jnp.mean(xc * xc, axis=-1, keepdims=True)
    return xc * lax.rsqrt(var + LN_EPS) * g + b


def _gelu_tanh(x):
    c = 0.7978845608028654
    return x * (0.5 * (1.0 + jnp.tanh(c * (x + 0.044715 * (x * x * x)))))


def _sigmoid(x):
    return 1.0 / (1.0 + jnp.exp(-x))


def _dot(a, b):
    return jnp.dot(a, b, preferred_element_type=F32)


def _resident(shape):
    nd = len(shape)
    return pl.BlockSpec(shape, lambda *_: (0,) * nd, pipeline_mode=pl.Buffered(1))


def _qkv_kernel(x_ref, wq_ref, wkt_ref, wv_ref, q_ref, kt_ref, v_ref):
    xb = x_ref[...].astype(BF16)
    q_ref[...] = (_dot(xb, wq_ref[...]) * 0.125).astype(BF16)
    v_ref[...] = _dot(xb, wv_ref[...]).astype(BF16)
    kt = lax.dot_general(wkt_ref[...], xb, (((1,), (1,)), ((), ())),
                         preferred_element_type=F32)
    kt_ref[0] = kt.astype(BF16)


def _qkv_proj(x2, wq, wkt, wv, batch, seq, tm):
    tokens = x2.shape[0]
    tiles_per_seq = seq // tm
    return pl.pallas_call(
        _qkv_kernel,
        grid=(tokens // tm,),
        in_specs=[
            pl.BlockSpec((tm, D_MODEL), lambda i: (i, 0)),
            _resident((D_MODEL, SB_WIDTH)),
            _resident((SB_WIDTH, D_MODEL)),
            _resident((D_MODEL, SB_WIDTH)),
        ],
        out_specs=[
            pl.BlockSpec((tm, SB_WIDTH), lambda i: (i, 0)),
            pl.BlockSpec((1, SB_WIDTH, tm), lambda i: (i // tiles_per_seq, 0, i % tiles_per_seq)),
            pl.BlockSpec((tm, SB_WIDTH), lambda i: (i, 0)),
        ],
        out_shape=[
            jax.ShapeDtypeStruct((tokens, SB_WIDTH), BF16),
            jax.ShapeDtypeStruct((batch, SB_WIDTH, seq), BF16),
            jax.ShapeDtypeStruct((tokens, SB_WIDTH), BF16),
        ],
        compiler_params=pltpu.CompilerParams(
            dimension_semantics=("arbitrary",), vmem_limit_bytes=VMEM_LIMIT),
        name="qkv_proj",
    )(x2, wq, wkt, wv)


def _softplus(z):
    return jnp.maximum(z, 0.0) + jnp.log(1.0 + jnp.exp(-jnp.abs(z)))


def _sb_attn_kernel(q_ref, kt_ref, v_ref, tri_ref, o_ref, kbd_ref, vbd_ref, pre_ref, rs_ref,
                    carry_ref, acc_ref, *, n_kblocks, sub):
    qs = pl.program_id(2)
    mq = sub * Q_BLOCK

    @pl.when(qs == 0)
    def _():
        row_lo = lax.broadcasted_iota(jnp.int32, (PAIR, Q_BLOCK), 0) < HEAD_DIM
        lane_lo = lax.broadcasted_iota(jnp.int32, (Q_BLOCK, PAIR), 1) < HEAD_DIM
        zero = jnp.zeros((PAIR, Q_BLOCK), BF16)
        for j in range(n_kblocks):
            kt = kt_ref[0, :, j * Q_BLOCK:(j + 1) * Q_BLOCK]
            kbd_ref[j, :, 0:Q_BLOCK] = jnp.where(row_lo, kt, zero)
            kbd_ref[j, :, Q_BLOCK:2 * Q_BLOCK] = jnp.where(row_lo, zero, kt)
            vt = v_ref[0, j * Q_BLOCK:(j + 1) * Q_BLOCK, :]
            vbd_ref[j, 0:Q_BLOCK, :] = jnp.where(lane_lo, vt, zero)
            vbd_ref[j, Q_BLOCK:2 * Q_BLOCK, :] = jnp.where(lane_lo, zero, vt)

    tri = tri_ref[...]
    row = lax.broadcasted_iota(jnp.int32, (Q_BLOCK, 2 * Q_BLOCK), 0)
    col = lax.broadcasted_iota(jnp.int32, (Q_BLOCK, 2 * Q_BLOCK), 1) & (Q_BLOCK - 1)
    causal = col < row

    def mask_top(x, fill, diag):
        if not diag:
            return x
        top = jnp.where(causal, x[:Q_BLOCK], fill)
        return top if x.shape[0] == Q_BLOCK else jnp.concatenate([top, x[Q_BLOCK:]], axis=0)

    def stage1(q2, j, diag):
        z = _dot(q2, kbd_ref[j])
        sp = mask_top(_softplus(z), 0.0, diag)
        hi = sp.astype(BF16)
        lo = (sp - hi.astype(F32)).astype(BF16)
        cr = [_dot(jnp.concatenate([hi[:, h * Q_BLOCK:(h + 1) * Q_BLOCK],
                                    lo[:, h * Q_BLOCK:(h + 1) * Q_BLOCK]], axis=1), tri)
              for h in range(2)]
        pre = z + jnp.concatenate([cr[0][:, :Q_BLOCK], cr[1][:, :Q_BLOCK]], axis=1)
        rs = jnp.concatenate([cr[0][:, Q_BLOCK:], cr[1][:, Q_BLOCK:]], axis=1)
        return pre, rs

    def stage2(pre, rs, r0, j, diag):
        c = carry_ref[r0:mq, :]
        carry_ref[r0:mq, :] = c + rs
        w = jnp.exp(mask_top(pre + c, -1e30, diag)).astype(BF16)
        acc_ref[r0:mq, :] += _dot(w, vbd_ref[j])

    carry_ref[...] = jnp.zeros_like(carry_ref)
    acc_ref[...] = jnp.zeros_like(acc_ref)
    j0 = qs * sub
    for d in reversed(range(sub)):
        r0 = d * Q_BLOCK
        pre, rs = stage1(q_ref[0, r0:mq, :], j0 + d, True)
        stage2(pre, rs, r0, j0 + d, True)

    @pl.when(qs > 0)
    def _():
        pre, rs = stage1(q_ref[0], j0 - 1, False)
        pre_ref[...] = pre
        rs_ref[...] = rs

        def body(i, c):
            j = j0 - 1 - i
            stage2(pre_ref[...], rs_ref[...], 0, j, False)
            pre_n, rs_n = stage1(q_ref[0], j - 1, False)
            pre_ref[...] = pre_n
            rs_ref[...] = rs_n
            return c

        lax.fori_loop(0, j0 - 1, body, 0)
        stage2(pre_ref[...], rs_ref[...], 0, 0, False)

    o_ref[0] = acc_ref[...].astype(BF16)


def _sb_attention(q3, kt3, v3, tri, sub):
    batch, seq, _ = q3.shape
    n_blocks = seq // Q_BLOCK
    mq = sub * Q_BLOCK
    return pl.pallas_call(
        functools.partial(_sb_attn_kernel, n_kblocks=n_blocks, sub=sub),
        grid=(batch, N_PAIRS, seq // mq),
        in_specs=[
            pl.BlockSpec((1, mq, PAIR), lambda b, p, i: (b, i, p)),
            pl.BlockSpec((1, PAIR, seq), lambda b, p, i: (b, p, 0)),
            pl.BlockSpec((1, seq, PAIR), lambda b, p, i: (b, 0, p)),
            _resident((2 * Q_BLOCK, 2 * Q_BLOCK)),
        ],
        out_specs=pl.BlockSpec((1, mq, PAIR), lambda b, p, i: (b, i, p)),
        out_shape=jax.ShapeDtypeStruct((batch, seq, SB_WIDTH), BF16),
        scratch_shapes=[
            pltpu.VMEM((n_blocks, PAIR, 2 * Q_BLOCK), BF16),
            pltpu.VMEM((n_blocks, 2 * Q_BLOCK, PAIR), BF16),
            pltpu.VMEM((mq, 2 * Q_BLOCK), F32),
            pltpu.VMEM((mq, 2 * Q_BLOCK), F32),
            pltpu.VMEM((mq, 2 * Q_BLOCK), F32),
            pltpu.VMEM((mq, PAIR), F32),
        ],
        compiler_params=pltpu.CompilerParams(
            dimension_semantics=("arbitrary", "arbitrary", "arbitrary"),
            vmem_limit_bytes=VMEM_LIMIT),
        name="sb_attn",
    )(q3, kt3, v3, tri)


def _mixer_out_kernel(x_ref, ya_ref, wu_ref, wvs_ref, wg_ref, bg_ref, lsg_ref, lsb_ref, wsp_ref,
                      bsp_ref, wa_ref, wb_ref, wo_ref, bo_ref, g1_ref, b1_ref, o_ref):
    x = x_ref[...]
    xb = x.astype(BF16)
    tm = x.shape[0]

    u = _gelu_tanh(_dot(xb, wu_ref[...]))
    vs = _ln(_gelu_tanh(_dot(xb, wvs_ref[...])), lsg_ref[...], lsb_ref[...]).astype(BF16)

    tril = (lax.broadcasted_iota(jnp.int32, (CHUNK, CHUNK), 1)
            <= lax.broadcasted_iota(jnp.int32, (CHUNK, CHUNK), 0))
    wsp = [jnp.where(tril, wsp_ref[g], 0.0).astype(BF16) for g in range(N_SG_GROUPS)]
    lane_lo = lax.broadcasted_iota(jnp.int32, (CHUNK, LANES), 1) < HEAD_DIM
    zero = jnp.zeros((CHUNK, LANES), BF16)
    rows = []
    for c in range(tm // CHUNK):
        parts = []
        for p in range(N_SG_GROUPS // 2):
            v2 = vs[c * CHUNK:(c + 1) * CHUNK, p * LANES:(p + 1) * LANES]
            parts.append(_dot(wsp[2 * p], jnp.where(lane_lo, v2, zero))
                         + _dot(wsp[2 * p + 1], jnp.where(lane_lo, zero, v2)))
        rows.append(jnp.concatenate(parts, axis=1) + bsp_ref[...])
    yb = (u * jnp.concatenate(rows, axis=0)).astype(BF16)

    g = _sigmoid(_dot(xb, wg_ref[...]) + bg_ref[...])
    merged = (g[:, :D_MODEL] * _dot(ya_ref[...], wa_ref[...])
              + g[:, D_MODEL:] * _dot(yb, wb_ref[...]))
    y = _dot(merged.astype(BF16), wo_ref[...]) + bo_ref[...]
    o_ref[...] = _ln(DN_ALPHA * x + y, g1_ref[...], b1_ref[...])


def _mixer_out(x2, ya2, wu, wvs, wg, bg, lsg, lsb, wsp, bsp, wa, wb, wo, bo, g1, b1, tm):
    tokens = x2.shape[0]
    return pl.pallas_call(
        _mixer_out_kernel,
        grid=(tokens // tm,),
        in_specs=[
            pl.BlockSpec((tm, D_MODEL), lambda i: (i, 0)),
            pl.BlockSpec((tm, SB_WIDTH), lambda i: (i, 0)),
            _resident(wu.shape), _resident(wvs.shape), _resident(wg.shape), _resident(bg.shape),
            _resident(lsg.shape), _resident(lsb.shape), _resident(wsp.shape), _resident(bsp.shape),
            _resident(wa.shape), _resident(wb.shape), _resident(wo.shape), _resident(bo.shape),
            _resident(g1.shape), _resident(b1.shape),
        ],
        out_specs=pl.BlockSpec((tm, D_MODEL), lambda i: (i, 0)),
        out_shape=jax.ShapeDtypeStruct((tokens, D_MODEL), F32),
        compiler_params=pltpu.CompilerParams(
            dimension_semantics=("arbitrary",), vmem_limit_bytes=VMEM_LIMIT),
        name="mixer_out",
    )(x2, ya2, wu, wvs, wg, bg, lsg, lsb, wsp, bsp, wa, wb, wo, bo, g1, b1)


def _conv_ffn_kernel(h_ref, wup_ref, cw_ref, cb_ref, wd_ref, bd_ref, g2_ref, b2_ref, o_ref, tail_ref,
                     *, tiles_per_seq):
    h = h_ref[...]
    hb = h.astype(BF16)
    tm = h.shape[0]
    seq_start = (pl.program_id(0) % tiles_per_seq) == 0
    row8 = lax.broadcasted_iota(jnp.int32, (8, FF_CHUNK), 0)

    def conv_cols(c0):
        sl = slice(c0, c0 + FF_CHUNK)
        up = _dot(hb, wup_ref[:, sl])
        prev = jnp.where(seq_start, 0.0, tail_ref[:, sl])
        tail_ref[:, sl] = up[tm - 8:tm]
        s1 = pltpu.roll(up, 1, axis=0)
        s2 = pltpu.roll(up, 2, axis=0)
        head = up[0:8]
        p1 = pltpu.roll(prev, 1, axis=0)
        p2 = pltpu.roll(prev, 2, axis=0)
        h1 = jnp.where(row8 < 1, p1, pltpu.roll(head, 1, axis=0))
        h2 = jnp.where(row8 < 2, p2, pltpu.roll(head, 2, axis=0))
        s1 = jnp.concatenate([h1, s1[8:]], axis=0)
        s2 = jnp.concatenate([h2, s2[8:]], axis=0)
        cw = cw_ref[:, sl]
        return cb_ref[:, sl] + cw[0:1] * s2 + cw[1:2] * s1 + cw[2:3] * up

    acc = jnp.zeros((tm, D_MODEL), F32)
    for c in range(D_FF // FF_CHUNK):
        a = conv_cols(c * FF_CHUNK)
        b = conv_cols(D_FF + c * FF_CHUNK)
        act = (a * _sigmoid(a) * b).astype(BF16)
        acc = acc + _dot(act, wd_ref[c * FF_CHUNK:(c + 1) * FF_CHUNK, :])
    y = acc + bd_ref[...]
    o_ref[...] = _ln(DN_ALPHA * h + y, g2_ref[...], b2_ref[...])


def _conv_ffn(h2, wup, cw, cb, wd, bd, g2, b2, seq, tm):
    tokens = h2.shape[0]
    return pl.pallas_call(
        functools.partial(_conv_ffn_kernel, tiles_per_seq=seq // tm),
        grid=(tokens // tm,),
        in_specs=[
            pl.BlockSpec((tm, D_MODEL), lambda i: (i, 0)),
            _resident(wup.shape), _resident(cw.shape), _resident(cb.shape), _resident(wd.shape),
            _resident(bd.shape), _resident(g2.shape), _resident(b2.shape),
        ],
        out_specs=pl.BlockSpec((tm, D_MODEL), lambda i: (i, 0)),
        out_shape=jax.ShapeDtypeStruct((tokens, D_MODEL), F32),
        scratch_shapes=[pltpu.VMEM((8, 2 * D_FF), F32)],
        compiler_params=pltpu.CompilerParams(
            dimension_semantics=("arbitrary",), vmem_limit_bytes=VMEM_LIMIT),
        name="conv_ffn",
    )(h2, wup, cw, cb, wd, bd, g2, b2)


def _tri_const():
    j = jnp.arange(Q_BLOCK)[:, None]
    s = jnp.arange(Q_BLOCK)[None, :]
    half = jnp.concatenate([-(j >= s).astype(F32), -jnp.ones((Q_BLOCK, Q_BLOCK), F32)], axis=1)
    return jnp.concatenate([half, half], axis=0).astype(BF16)


def _layer(h, w_in, b_gate, ln_sg_g, ln_sg_b, w_spatial, b_spatial, w_branch_a, w_branch_b, w_out,
           b_out, ln1_g, ln1_b, w_up, conv_w, conv_b, w_down, b_down, ln2_g, ln2_b):
    batch, seq, _ = h.shape
    tokens = batch * seq
    x2 = h.reshape(tokens, D_MODEL)
    row = lambda v: v.reshape(1, -1)

    wq = w_in[:, 0:SB_WIDTH].astype(BF16)
    wkt = w_in[:, SB_WIDTH:2 * SB_WIDTH].T.astype(BF16)
    wv = w_in[:, 2 * SB_WIDTH:3 * SB_WIDTH].astype(BF16)
    wu = w_in[:, 3 * SB_WIDTH:3 * SB_WIDTH + SG_WIDTH].astype(BF16)
    wvs = w_in[:, 3 * SB_WIDTH + SG_WIDTH:3 * SB_WIDTH + 2 * SG_WIDTH].astype(BF16)
    wg = w_in[:, 3 * SB_WIDTH + 2 * SG_WIDTH:].astype(BF16)
    bsp = jnp.repeat(b_spatial.T, SG_WIDTH // N_SG_GROUPS, axis=1)

    q2, kt3, v2 = _qkv_proj(x2, wq, wkt, wv, batch, seq, tm=512)
    ya3 = _sb_attention(q2.reshape(batch, seq, SB_WIDTH), kt3, v2.reshape(batch, seq, SB_WIDTH),
                        _tri_const(), sub=4)
    h1 = _mixer_out(x2, ya3.reshape(tokens, SB_WIDTH), wu, wvs, wg, row(b_gate), row(ln_sg_g),
                    row(ln_sg_b), w_spatial, bsp, w_branch_a.astype(BF16), w_branch_b.astype(BF16),
                    w_out.astype(BF16), row(b_out), row(ln1_g), row(ln1_b), tm=512)
    h2 = _conv_ffn(h1, w_up.astype(BF16), conv_w, row(conv_b), w_down.astype(BF16), row(b_down),
                   row(ln2_g), row(ln2_b), seq, tm=512)
    return h2.reshape(batch, seq, D_MODEL)


def kernel(x, w_in, b_gate, ln_sg_g, ln_sg_b, w_spatial, b_spatial, w_branch_a, w_branch_b, w_out,
           b_out, ln1_g, ln1_b, w_up, conv_w, conv_b, w_down, b_down, ln2_g, ln2_b):
    h = x
    for l in range(w_in.shape[0]):
        h = _layer(h, w_in[l], b_gate[l], ln_sg_g[l], ln_sg_b[l], w_spatial[l], b_spatial[l],
                   w_branch_a[l], w_branch_b[l], w_out[l], b_out[l], ln1_g[l], ln1_b[l], w_up[l],
                   conv_w[l], conv_b[l], w_down[l], b_down[l], ln2_g[l], ln2_b[l])
    return h
```

```python
import functools

import jax
import jax.numpy as jnp
from jax import lax
from jax.experimental import pallas as pl
from jax.experimental.pallas import tpu as pltpu

F32 = jnp.float32
BF16 = jnp.bfloat16

D_MODEL = 1024
N_HEADS = 8
HEAD_DIM = 64
SB_WIDTH = N_HEADS * HEAD_DIM
SG_WIDTH = 512
N_SG_GROUPS = 8
CHUNK = 128
Q_BLOCK = 128
D_FF = 2816
LN_EPS = 1e-5
STICK_CUTOFF = 64.0
DN_ALPHA = 2.0 ** 0.25

LANES = 128
PAIR = 2 * HEAD_DIM
N_PAIRS = N_HEADS // 2
FF_PAIR = 256
FF_STEPS = D_FF // FF_PAIR
FF_SLABS = 2 * FF_PAIR // LANES
VMEM_LIMIT = 56 * 1024 * 1024


def _ln(x, g, b):
    mu = jnp.mean(x, axis=-1, keepdims=True)
    xc = x - mu
    var = jnp.mean(xc * xc, axis=-1, keepdims=True)
    return xc * lax.rsqrt(var + LN_EPS) * g + b


def _gelu_tanh(x):
    c = 0.7978845608028654
    return x * (0.5 * (1.0 + jnp.tanh(c * (x + 0.044715 * (x * x * x)))))


def _sigmoid(x):
    return 1.0 / (1.0 + jnp.exp(-x))


def _dot(a, b):
    return jnp.dot(a, b, preferred_element_type=F32)


def _resident(shape):
    nd = len(shape)
    return pl.BlockSpec(shape, lambda *_: (0,) * nd, pipeline_mode=pl.Buffered(1))


def _qkv_kernel(x_ref, wq_ref, wkt_ref, wv_ref, q_ref, kt_ref, v_ref):
    xb = x_ref[...].astype(BF16)
    q_ref[...] = (_dot(xb, wq_ref[...]) * 0.125).astype(BF16)
    v_ref[...] = _dot(xb, wv_ref[...]).astype(BF16)
    kt = lax.dot_general(wkt_ref[...], xb, (((1,), (1,)), ((), ())),
                         preferred_element_type=F32)
    kt_ref[0] = kt.astype(BF16)


def _qkv_proj(x2, wq, wkt, wv, batch, seq, tm):
    tokens = x2.shape[0]
    tiles_per_seq = seq // tm
    return pl.pallas_call(
        _qkv_kernel,
        grid=(tokens // tm,),
        in_specs=[
            pl.BlockSpec((tm, D_MODEL), lambda i: (i, 0)),
            _resident((D_MODEL, SB_WIDTH)),
            _resident((SB_WIDTH, D_MODEL)),
            _resident((D_MODEL, SB_WIDTH)),
        ],
        out_specs=[
            pl.BlockSpec((tm, SB_WIDTH), lambda i: (i, 0)),
            pl.BlockSpec((1, SB_WIDTH, tm), lambda i: (i // tiles_per_seq, 0, i % tiles_per_seq)),
            pl.BlockSpec((tm, SB_WIDTH), lambda i: (i, 0)),
        ],
        out_shape=[
            jax.ShapeDtypeStruct((tokens, SB_WIDTH), BF16),
            jax.ShapeDtypeStruct((batch, SB_WIDTH, seq), BF16),
            jax.ShapeDtypeStruct((tokens, SB_WIDTH), BF16),
        ],
        compiler_params=pltpu.CompilerParams(
            dimension_semantics=("arbitrary",), vmem_limit_bytes=VMEM_LIMIT),
        name="qkv_proj",
    )(x2, wq, wkt, wv)


def _softplus(z):
    return jnp.maximum(z, 0.0) + jnp.log(1.0 + jnp.exp(-jnp.abs(z)))


def _sb_attn_kernel(q_ref, kt_ref, v_ref, tri_ref, o_ref, kbd_ref, vbd_ref, pre_ref, rs_ref,
                    carry_ref, acc_ref, *, n_kblocks, sub):
    qs = pl.program_id(2)
    mq = sub * Q_BLOCK

    @pl.when(qs == 0)
    def _():
        row_lo = lax.broadcasted_iota(jnp.int32, (PAIR, Q_BLOCK), 0) < HEAD_DIM
        lane_lo = lax.broadcasted_iota(jnp.int32, (Q_BLOCK, PAIR), 1) < HEAD_DIM
        zero = jnp.zeros((PAIR, Q_BLOCK), BF16)
        for j in range(n_kblocks):
            kt = kt_ref[0, :, j * Q_BLOCK:(j + 1) * Q_BLOCK]
            kbd_ref[j, :, 0:Q_BLOCK] = jnp.where(row_lo, kt, zero)
            kbd_ref[j, :, Q_BLOCK:2 * Q_BLOCK] = jnp.where(row_lo, zero, kt)
            vt = v_ref[0, j * Q_BLOCK:(j + 1) * Q_BLOCK, :]
            vbd_ref[j, 0:Q_BLOCK, :] = jnp.where(lane_lo, vt, zero)
            vbd_ref[j, Q_BLOCK:2 * Q_BLOCK, :] = jnp.where(lane_lo, zero, vt)

    tri = tri_ref[...]
    row = lax.broadcasted_iota(jnp.int32, (Q_BLOCK, 2 * Q_BLOCK), 0)
    col = lax.broadcasted_iota(jnp.int32, (Q_BLOCK, 2 * Q_BLOCK), 1) & (Q_BLOCK - 1)
    causal = col < row

    def mask_top(x, fill, diag):
        if not diag:
            return x
        top = jnp.where(causal, x[:Q_BLOCK], fill)
        return top if x.shape[0] == Q_BLOCK else jnp.concatenate([top, x[Q_BLOCK:]], axis=0)

    def stage1(q2, j, diag):
        z = _dot(q2, kbd_ref[j])
        sp = mask_top(_softplus(z), 0.0, diag)
        hi = sp.astype(BF16)
        lo = (sp - hi.astype(F32)).astype(BF16)
        cr = [_dot(jnp.concatenate([hi[:, h * Q_BLOCK:(h + 1) * Q_BLOCK],
                                    lo[:, h * Q_BLOCK:(h + 1) * Q_BLOCK]], axis=1), tri)
              for h in range(2)]
        pre = z + jnp.concatenate([cr[0][:, :Q_BLOCK], cr[1][:, :Q_BLOCK]], axis=1)
        rs = jnp.concatenate([cr[0][:, Q_BLOCK:], cr[1][:, Q_BLOCK:]], axis=1)
        return pre, rs

    def stage2(pre, rs, r0, j, diag):
        c = carry_ref[r0:mq, :]
        c_new = c + rs
        carry_ref[r0:mq, :] = c_new
        w = jnp.exp(mask_top(pre + c, -1e30, diag)).astype(BF16)
        acc_ref[r0:mq, :] += _dot(w, vbd_ref[j])
        return c_new

    carry_ref[...] = jnp.zeros_like(carry_ref)
    acc_ref[...] = jnp.zeros_like(acc_ref)
    j0 = qs * sub
    for d in reversed(range(sub)):
        r0 = d * Q_BLOCK
        pre, rs = stage1(q_ref[0, r0:mq, :], j0 + d, True)
        stage2(pre, rs, r0, j0 + d, True)

    @pl.when(qs > 0)
    def _():
        pre, rs = stage1(q_ref[0], j0 - 1, False)
        pre_ref[...] = pre
        rs_ref[...] = rs

        def cond(state):
            j, alive = state
            return jnp.logical_and(j >= 1, alive > 0)

        def body(state):
            j, _ = state
            c_new = stage2(pre_ref[...], rs_ref[...], 0, j, False)
            pre_n, rs_n = stage1(q_ref[0], j - 1, False)
            pre_ref[...] = pre_n
            rs_ref[...] = rs_n
            return j - 1, (jnp.max(c_new) >= -STICK_CUTOFF).astype(jnp.int32)

        j_end, alive = lax.while_loop(cond, body, (j0 - 1, jnp.int32(1)))

        @pl.when(jnp.logical_and(j_end == 0, alive > 0))
        def _():
            stage2(pre_ref[...], rs_ref[...], 0, 0, False)

    o_ref[0] = acc_ref[...].astype(BF16)


def _sb_attention(q3, kt3, v3, tri, sub):
    batch, seq, _ = q3.shape
    n_blocks = seq // Q_BLOCK
    mq = sub * Q_BLOCK
    return pl.pallas_call(
        functools.partial(_sb_attn_kernel, n_kblocks=n_blocks, sub=sub),
        grid=(batch, N_PAIRS, seq // mq),
        in_specs=[
            pl.BlockSpec((1, mq, PAIR), lambda b, p, i: (b, i, p)),
            pl.BlockSpec((1, PAIR, seq), lambda b, p, i: (b, p, 0)),
            pl.BlockSpec((1, seq, PAIR), lambda b, p, i: (b, 0, p)),
            _resident((2 * Q_BLOCK, 2 * Q_BLOCK)),
        ],
        out_specs=pl.BlockSpec((1, mq, PAIR), lambda b, p, i: (b, i, p)),
        out_shape=jax.ShapeDtypeStruct((batch, seq, SB_WIDTH), BF16),
        scratch_shapes=[
            pltpu.VMEM((n_blocks, PAIR, 2 * Q_BLOCK), BF16),
            pltpu.VMEM((n_blocks, 2 * Q_BLOCK, PAIR), BF16),
            pltpu.VMEM((mq, 2 * Q_BLOCK), F32),
            pltpu.VMEM((mq, 2 * Q_BLOCK), F32),
            pltpu.VMEM((mq, 2 * Q_BLOCK), F32),
            pltpu.VMEM((mq, PAIR), F32),
        ],
        compiler_params=pltpu.CompilerParams(
            dimension_semantics=("arbitrary", "arbitrary", "arbitrary"),
            vmem_limit_bytes=VMEM_LIMIT),
        name="sb_attn",
    )(q3, kt3, v3, tri)


def _mixer_out_kernel(x_ref, ya_ref, wu_ref, wvs_ref, wg_ref, bg_ref, lsg_ref, lsb_ref, wsp_ref,
                      bsp_ref, wa_ref, wb_ref, wo_ref, bo_ref, g1_ref, b1_ref, o_ref):
    x = x_ref[...]
    xb = x.astype(BF16)
    tm = x.shape[0]

    u = _gelu_tanh(_dot(xb, wu_ref[...]))
    vs = _ln(_gelu_tanh(_dot(xb, wvs_ref[...])), lsg_ref[...], lsb_ref[...]).astype(BF16)

    tril = (lax.broadcasted_iota(jnp.int32, (CHUNK, CHUNK), 1)
            <= lax.broadcasted_iota(jnp.int32, (CHUNK, CHUNK), 0))
    wsp = [jnp.where(tril, wsp_ref[g], 0.0).astype(BF16) for g in range(N_SG_GROUPS)]
    lane_lo = lax.broadcasted_iota(jnp.int32, (CHUNK, LANES), 1) < HEAD_DIM
    zero = jnp.zeros((CHUNK, LANES), BF16)
    rows = []
    for c in range(tm // CHUNK):
        parts = []
        for p in range(N_SG_GROUPS // 2):
            v2 = vs[c * CHUNK:(c + 1) * CHUNK, p * LANES:(p + 1) * LANES]
            parts.append(_dot(wsp[2 * p], jnp.where(lane_lo, v2, zero))
                         + _dot(wsp[2 * p + 1], jnp.where(lane_lo, zero, v2)))
        rows.append(jnp.concatenate(parts, axis=1) + bsp_ref[...])
    yb = (u * jnp.concatenate(rows, axis=0)).astype(BF16)

    g = _sigmoid(_dot(xb, wg_ref[...]) + bg_ref[...])
    merged = (g[:, :D_MODEL] * _dot(ya_ref[...], wa_ref[...])
              + g[:, D_MODEL:] * _dot(yb, wb_ref[...]))
    y = _dot(merged.astype(BF16), wo_ref[...]) + bo_ref[...]
    o_ref[...] = _ln(DN_ALPHA * x + y, g1_ref[...], b1_ref[...])


def _mixer_out(x2, ya2, wu, wvs, wg, bg, lsg, lsb, wsp, bsp, wa, wb, wo, bo, g1, b1, tm):
    tokens = x2.shape[0]
    return pl.pallas_call(
        _mixer_out_kernel,
        grid=(tokens // tm,),
        in_specs=[
            pl.BlockSpec((tm, D_MODEL), lambda i: (i, 0)),
            pl.BlockSpec((tm, SB_WIDTH), lambda i: (i, 0)),
            _resident(wu.shape), _resident(wvs.shape), _resident(wg.shape), _resident(bg.shape),
            _resident(lsg.shape), _resident(lsb.shape), _resident(wsp.shape), _resident(bsp.shape),
            _resident(wa.shape), _resident(wb.shape), _resident(wo.shape), _resident(bo.shape),
            _resident(g1.shape), _resident(b1.shape),
        ],
        out_specs=pl.BlockSpec((tm, D_MODEL), lambda i: (i, 0)),
        out_shape=jax.ShapeDtypeStruct((tokens, D_MODEL), F32),
        compiler_params=pltpu.CompilerParams(
            dimension_semantics=("arbitrary",), vmem_limit_bytes=VMEM_LIMIT),
        name="mixer_out",
    )(x2, ya2, wu, wvs, wg, bg, lsg, lsb, wsp, bsp, wa, wb, wo, bo, g1, b1)


def _conv_ffn_kernel(h_ref, wab_ref, cw_ref, cb_ref, wd_ref, bd_ref, g2_ref, b2_ref, o_ref,
                     hb_ref, slab_ref, tail_ref, act_ref, *, tiles_per_seq):
    tm = h_ref.shape[0]
    hb_ref[...] = h_ref[...].astype(BF16)
    seq_start = (pl.program_id(0) % tiles_per_seq) == 0

    def produce(c, slot):
        up = _dot(hb_ref[...], wab_ref[c])
        for s in range(FF_SLABS):
            slab_ref[slot, s, 8:tm + 8, :] = up[:, s * LANES:(s + 1) * LANES]

    def consume(c, slot):
        cw = cw_ref[c]
        cb = cb_ref[c]
        conv = []
        for s in range(FF_SLABS):
            sl = slice(s * LANES, (s + 1) * LANES)
            slab_ref[slot, s, 0:8, :] = jnp.where(seq_start, 0.0, tail_ref[c, s])
            tail_ref[c, s] = slab_ref[slot, s, tm:tm + 8, :]
            conv.append(cb[:, sl]
                        + cw[0:1, sl] * slab_ref[slot, s, 6:tm + 6, :]
                        + cw[1:2, sl] * slab_ref[slot, s, 7:tm + 7, :]
                        + cw[2:3, sl] * slab_ref[slot, s, 8:tm + 8, :])
        half = FF_SLABS // 2
        for s in range(half):
            a, b = conv[s], conv[half + s]
            act_ref[c, :, s * LANES:(s + 1) * LANES] = (a * _sigmoid(a) * b).astype(BF16)

    produce(0, 0)

    def body(k, carry):
        c = 2 * k
        produce(c + 1, 1)
        consume(c, 0)
        produce(c + 2, 0)
        consume(c + 1, 1)
        return carry

    lax.fori_loop(0, (FF_STEPS - 1) // 2, body, 0)
    consume(FF_STEPS - 1, 0)

    y = bd_ref[...]
    for c in range(FF_STEPS):
        y = y + _dot(act_ref[c], wd_ref[c])
    o_ref[...] = _ln(DN_ALPHA * h_ref[...] + y, g2_ref[...], b2_ref[...])


def _conv_ffn(h2, wab, cw, cb, wd, bd, g2, b2, seq, tm):
    tokens = h2.shape[0]
    return pl.pallas_call(
        functools.partial(_conv_ffn_kernel, tiles_per_seq=seq // tm),
        grid=(tokens // tm,),
        in_specs=[
            pl.BlockSpec((tm, D_MODEL), lambda i: (i, 0)),
            _resident(wab.shape), _resident(cw.shape), _resident(cb.shape), _resident(wd.shape),
            _resident(bd.shape), _resident(g2.shape), _resident(b2.shape),
        ],
        out_specs=pl.BlockSpec((tm, D_MODEL), lambda i: (i, 0)),
        out_shape=jax.ShapeDtypeStruct((tokens, D_MODEL), F32),
        scratch_shapes=[
            pltpu.VMEM((tm, D_MODEL), BF16),
            pltpu.VMEM((2, FF_SLABS, tm + 8, LANES), F32),
            pltpu.VMEM((FF_STEPS, FF_SLABS, 8, LANES), F32),
            pltpu.VMEM((FF_STEPS, tm, FF_PAIR), BF16),
        ],
        compiler_params=pltpu.CompilerParams(
            dimension_semantics=("arbitrary",), vmem_limit_bytes=VMEM_LIMIT),
        name="conv_ffn",
    )(h2, wab, cw, cb, wd, bd, g2, b2)


def _ffn_step_major(w):
    lead = w.shape[:-1]
    a = w[..., :D_FF].reshape(*lead, FF_STEPS, FF_PAIR)
    b = w[..., D_FF:].reshape(*lead, FF_STEPS, FF_PAIR)
    return jnp.moveaxis(jnp.concatenate([a, b], axis=-1), -2, 0)


def _tri_const():
    j = jnp.arange(Q_BLOCK)[:, None]
    s = jnp.arange(Q_BLOCK)[None, :]
    half = jnp.concatenate([-(j >= s).astype(F32), -jnp.ones((Q_BLOCK, Q_BLOCK), F32)], axis=1)
    return jnp.concatenate([half, half], axis=0).astype(BF16)


def _layer(h, w_in, b_gate, ln_sg_g, ln_sg_b, w_spatial, b_spatial, w_branch_a, w_branch_b, w_out,
           b_out, ln1_g, ln1_b, w_up, conv_w, conv_b, w_down, b_down, ln2_g, ln2_b):
    batch, seq, _ = h.shape
    tokens = batch * seq
    x2 = h.reshape(tokens, D_MODEL)
    row = lambda v: v.reshape(1, -1)

    wq = w_in[:, 0:SB_WIDTH].astype(BF16)
    wkt = w_in[:, SB_WIDTH:2 * SB_WIDTH].T.astype(BF16)
    wv = w_in[:, 2 * SB_WIDTH:3 * SB_WIDTH].astype(BF16)
    wu = w_in[:, 3 * SB_WIDTH:3 * SB_WIDTH + SG_WIDTH].astype(BF16)
    wvs = w_in[:, 3 * SB_WIDTH + SG_WIDTH:3 * SB_WIDTH + 2 * SG_WIDTH].astype(BF16)
    wg = w_in[:, 3 * SB_WIDTH + 2 * SG_WIDTH:].astype(BF16)
    bsp = jnp.repeat(b_spatial.T, SG_WIDTH // N_SG_GROUPS, axis=1)

    q2, kt3, v2 = _qkv_proj(x2, wq, wkt, wv, batch, seq, tm=512)
    ya3 = _sb_attention(q2.reshape(batch, seq, SB_WIDTH), kt3, v2.reshape(batch, seq, SB_WIDTH),
                        _tri_const(), sub=4)
    h1 = _mixer_out(x2, ya3.reshape(tokens, SB_WIDTH), wu, wvs, wg, row(b_gate), row(ln_sg_g),
                    row(ln_sg_b), w_spatial, bsp, w_branch_a.astype(BF16), w_branch_b.astype(BF16),
                    w_out.astype(BF16), row(b_out), row(ln1_g), row(ln1_b), tm=512)
    h2 = _conv_ffn(h1, _ffn_step_major(w_up).astype(BF16), _ffn_step_major(conv_w),
                   _ffn_step_major(row(conv_b)), w_down.astype(BF16).reshape(FF_STEPS, FF_PAIR, D_MODEL),
                   row(b_down), row(ln2_g), row(ln2_b), seq, tm=512)
    return h2.reshape(batch, seq, D_MODEL)


def kernel(x, w_in, b_gate, ln_sg_g, ln_sg_b, w_spatial, b_spatial, w_branch_a, w_branch_b, w_out,
           b_out, ln1_g, ln1_b, w_up, conv_w, conv_b, w_down, b_down, ln2_g, ln2_b):
    h = x
    for l in range(w_in.shape[0]):
        h = _layer(h, w_in[l], b_gate[l], ln_sg_g[l], ln_sg_b[l], w_spatial[l], b_spatial[l],
                   w_branch_a[l], w_branch_b[l], w_out[l], b_out[l], ln1_g[l], ln1_b[l], w_up[l],
                   conv_w[l], conv_b[l], w_down[l], b_down[l], ln2_g[l], ln2_b[l])
    return h
```

```python
import functools

import jax
import jax.numpy as jnp
from jax import lax
from jax.experimental import pallas as pl
from jax.experimental.pallas import tpu as pltpu

F32 = jnp.float32
BF16 = jnp.bfloat16

D_MODEL = 1024
N_HEADS = 8
HEAD_DIM = 64
SB_WIDTH = N_HEADS * HEAD_DIM
SG_WIDTH = 512
N_SG_GROUPS = 8
CHUNK = 128
Q_BLOCK = 128
D_FF = 2816
LN_EPS = 1e-5
STICK_CUTOFF = 64.0
DN_ALPHA = 2.0 ** 0.25

LANES = 128
PAIR = 2 * HEAD_DIM
N_PAIRS = N_HEADS // 2
FF_PAIR = 256
FF_STEPS = D_FF // FF_PAIR
FF_SLABS = 2 * FF_PAIR // LANES
VMEM_LIMIT = 56 * 1024 * 1024


def _ln(x, g, b):
    mu = jnp.mean(x, axis=-1, keepdims=True)
    xc = x - mu
    var = jnp.mean(xc * xc, axis=-1, keepdims=True)
    return xc * lax.rsqrt(var + LN_EPS) * g + b


def _gelu_tanh(x):
    c = 0.7978845608028654
    return x * (0.5 * (1.0 + jnp.tanh(c * (x + 0.044715 * (x * x * x)))))


def _sigmoid(x):
    return 1.0 / (1.0 + jnp.exp(-x))


def _dot(a, b):
    return jnp.dot(a, b, preferred_element_type=F32)


def _resident(shape):
    nd = len(shape)
    return pl.BlockSpec(shape, lambda *_: (0,) * nd, pipeline_mode=pl.Buffered(1))


def _qkv_kernel(x_ref, wq_ref, wkt_ref, wv_ref, q_ref, kt_ref, v_ref):
    xb = x_ref[...].astype(BF16)
    q_ref[...] = (_dot(xb, wq_ref[...]) * 0.125).astype(BF16)
    v_ref[...] = _dot(xb, wv_ref[...]).astype(BF16)
    kt = lax.dot_general(wkt_ref[...], xb, (((1,), (1,)), ((), ())),
                         preferred_element_type=F32)
    kt_ref[0] = kt.astype(BF16)


def _qkv_proj(x2, wq, wkt, wv, batch, seq, tm):
    tokens = x2.shape[0]
    tiles_per_seq = seq // tm
    return pl.pallas_call(
        _qkv_kernel,
        grid=(tokens // tm,),
        in_specs=[
            pl.BlockSpec((tm, D_MODEL), lambda i: (i, 0)),
            _resident((D_MODEL, SB_WIDTH)),
            _resident((SB_WIDTH, D_MODEL)),
            _resident((D_MODEL, SB_WIDTH)),
        ],
        out_specs=[
            pl.BlockSpec((tm, SB_WIDTH), lambda i: (i, 0)),
            pl.BlockSpec((1, SB_WIDTH, tm), lambda i: (i // tiles_per_seq, 0, i % tiles_per_seq)),
            pl.BlockSpec((tm, SB_WIDTH), lambda i: (i, 0)),
        ],
        out_shape=[
            jax.ShapeDtypeStruct((tokens, SB_WIDTH), BF16),
            jax.ShapeDtypeStruct((batch, SB_WIDTH, seq), BF16),
            jax.ShapeDtypeStruct((tokens, SB_WIDTH), BF16),
        ],
        compiler_params=pltpu.CompilerParams(
            dimension_semantics=("arbitrary",), vmem_limit_bytes=VMEM_LIMIT),
        name="qkv_proj",
    )(x2, wq, wkt, wv)


def _softplus(z):
    return jnp.maximum(z, 0.0) + jnp.log(1.0 + jnp.exp(-jnp.abs(z)))


def _sb_attn_kernel(q_ref, kt_ref, v_ref, tri_ref, o_ref, kbd_ref, vbd_ref, pre_ref, rs_ref,
                    carry_ref, acc_ref, *, n_kblocks, sub):
    qs = pl.program_id(2)
    mq = sub * Q_BLOCK

    @pl.when(qs == 0)
    def _():
        row_lo = lax.broadcasted_iota(jnp.int32, (PAIR, Q_BLOCK), 0) < HEAD_DIM
        lane_lo = lax.broadcasted_iota(jnp.int32, (Q_BLOCK, PAIR), 1) < HEAD_DIM
        zero = jnp.zeros((PAIR, Q_BLOCK), BF16)
        for j in range(n_kblocks):
            kt = kt_ref[0, :, j * Q_BLOCK:(j + 1) * Q_BLOCK]
            kbd_ref[j, :, 0:Q_BLOCK] = jnp.where(row_lo, kt, zero)
            kbd_ref[j, :, Q_BLOCK:2 * Q_BLOCK] = jnp.where(row_lo, zero, kt)
            vt = v_ref[0, j * Q_BLOCK:(j + 1) * Q_BLOCK, :]
            vbd_ref[j, 0:Q_BLOCK, :] = jnp.where(lane_lo, vt, zero)
            vbd_ref[j, Q_BLOCK:2 * Q_BLOCK, :] = jnp.where(lane_lo, zero, vt)

    tri = tri_ref[...]
    row = lax.broadcasted_iota(jnp.int32, (Q_BLOCK, 2 * Q_BLOCK), 0)
    col = lax.broadcasted_iota(jnp.int32, (Q_BLOCK, 2 * Q_BLOCK), 1) & (Q_BLOCK - 1)
    causal = col < row

    def mask_top(x, fill, diag):
        if not diag:
            return x
        top = jnp.where(causal, x[:Q_BLOCK], fill)
        return top if x.shape[0] == Q_BLOCK else jnp.concatenate([top, x[Q_BLOCK:]], axis=0)

    def stage1(q2, j, diag):
        z = _dot(q2, kbd_ref[j])
        sp = mask_top(_softplus(z), 0.0, diag)
        hi = sp.astype(BF16)
        lo = (sp - hi.astype(F32)).astype(BF16)
        cr = [_dot(jnp.concatenate([hi[:, h * Q_BLOCK:(h + 1) * Q_BLOCK],
                                    lo[:, h * Q_BLOCK:(h + 1) * Q_BLOCK]], axis=1), tri)
              for h in range(2)]
        pre = z + jnp.concatenate([cr[0][:, :Q_BLOCK], cr[1][:, :Q_BLOCK]], axis=1)
        rs = jnp.concatenate([cr[0][:, Q_BLOCK:], cr[1][:, Q_BLOCK:]], axis=1)
        return pre, rs

    def stage2(pre, rs, r0, j, diag, valid=None):
        c = carry_ref[r0:mq, :]
        e = pre + c
        if valid is not None:
            rs = jnp.where(valid, rs, 0.0)
            e = jnp.where(valid, e, -1e30)
        c_new = c + rs
        carry_ref[r0:mq, :] = c_new
        w = jnp.exp(mask_top(e, -1e30, diag)).astype(BF16)
        acc_ref[r0:mq, :] += _dot(w, vbd_ref[j])
        return c_new

    carry_ref[...] = jnp.zeros_like(carry_ref)
    acc_ref[...] = jnp.zeros_like(acc_ref)
    j0 = qs * sub
    for d in reversed(range(sub)):
        r0 = d * Q_BLOCK
        pre, rs = stage1(q_ref[0, r0:mq, :], j0 + d, True)
        stage2(pre, rs, r0, j0 + d, True)
    has_older = qs > 0
    j1 = jnp.maximum(j0 - 1, 0)
    pre, rs = stage1(q_ref[0], j1, False)
    c_new = stage2(pre, rs, 0, j1, False, valid=has_older)

    alive0 = jnp.max(c_new) >= -STICK_CUTOFF

    @pl.when(jnp.logical_and(j0 >= 2, alive0))
    def _():
        pre, rs = stage1(q_ref[0], j0 - 2, False)
        pre_ref[...] = pre
        rs_ref[...] = rs

        def cond(state):
            j, alive = state
            return jnp.logical_and(j >= 1, alive > 0)

        def body(state):
            j, _ = state
            c_j = stage2(pre_ref[...], rs_ref[...], 0, j, False)
            pre_n, rs_n = stage1(q_ref[0], j - 1, False)
            pre_ref[...] = pre_n
            rs_ref[...] = rs_n
            return j - 1, (jnp.max(c_j) >= -STICK_CUTOFF).astype(jnp.int32)

        j_end, alive = lax.while_loop(cond, body, (j0 - 2, jnp.int32(1)))

        @pl.when(jnp.logical_and(j_end == 0, alive > 0))
        def _():
            stage2(pre_ref[...], rs_ref[...], 0, 0, False)

    o_ref[0] = acc_ref[...].astype(BF16)


def _sb_attention(q3, kt3, v3, tri, sub):
    batch, seq, _ = q3.shape
    n_blocks = seq // Q_BLOCK
    mq = sub * Q_BLOCK
    return pl.pallas_call(
        functools.partial(_sb_attn_kernel, n_kblocks=n_blocks, sub=sub),
        grid=(batch, N_PAIRS, seq // mq),
        in_specs=[
            pl.BlockSpec((1, mq, PAIR), lambda b, p, i: (b, i, p)),
            pl.BlockSpec((1, PAIR, seq), lambda b, p, i: (b, p, 0)),
            pl.BlockSpec((1, seq, PAIR), lambda b, p, i: (b, 0, p)),
            _resident((2 * Q_BLOCK, 2 * Q_BLOCK)),
        ],
        out_specs=pl.BlockSpec((1, mq, PAIR), lambda b, p, i: (b, i, p)),
        out_shape=jax.ShapeDtypeStruct((batch, seq, SB_WIDTH), BF16),
        scratch_shapes=[
            pltpu.VMEM((n_blocks, PAIR, 2 * Q_BLOCK), BF16),
            pltpu.VMEM((n_blocks, 2 * Q_BLOCK, PAIR), BF16),
            pltpu.VMEM((mq, 2 * Q_BLOCK), F32),
            pltpu.VMEM((mq, 2 * Q_BLOCK), F32),
            pltpu.VMEM((mq, 2 * Q_BLOCK), F32),
            pltpu.VMEM((mq, PAIR), F32),
        ],
        compiler_params=pltpu.CompilerParams(
            dimension_semantics=("arbitrary", "arbitrary", "arbitrary"),
            vmem_limit_bytes=VMEM_LIMIT),
        name="sb_attn",
    )(q3, kt3, v3, tri)


def _mixer_out_kernel(x_ref, ya_ref, wu_ref, wvs_ref, wg_ref, bg_ref, lsg_ref, lsb_ref, wsp_ref,
                      bsp_ref, wa_ref, wb_ref, wo_ref, bo_ref, g1_ref, b1_ref, o_ref):
    x = x_ref[...]
    xb = x.astype(BF16)
    tm = x.shape[0]

    u = _gelu_tanh(_dot(xb, wu_ref[...]))
    vs = _ln(_gelu_tanh(_dot(xb, wvs_ref[...])), lsg_ref[...], lsb_ref[...]).astype(BF16)

    tril = (lax.broadcasted_iota(jnp.int32, (CHUNK, CHUNK), 1)
            <= lax.broadcasted_iota(jnp.int32, (CHUNK, CHUNK), 0))
    wsp = [jnp.where(tril, wsp_ref[g], 0.0).astype(BF16) for g in range(N_SG_GROUPS)]
    lane_lo = lax.broadcasted_iota(jnp.int32, (CHUNK, LANES), 1) < HEAD_DIM
    zero = jnp.zeros((CHUNK, LANES), BF16)
    rows = []
    for c in range(tm // CHUNK):
        parts = []
        for p in range(N_SG_GROUPS // 2):
            v2 = vs[c * CHUNK:(c + 1) * CHUNK, p * LANES:(p + 1) * LANES]
            parts.append(_dot(wsp[2 * p], jnp.where(lane_lo, v2, zero))
                         + _dot(wsp[2 * p + 1], jnp.where(lane_lo, zero, v2)))
        rows.append(jnp.concatenate(parts, axis=1) + bsp_ref[...])
    yb = (u * jnp.concatenate(rows, axis=0)).astype(BF16)

    g = _sigmoid(_dot(xb, wg_ref[...]) + bg_ref[...])
    merged = (g[:, :D_MODEL] * _dot(ya_ref[...], wa_ref[...])
              + g[:, D_MODEL:] * _dot(yb, wb_ref[...]))
    y = _dot(merged.astype(BF16), wo_ref[...]) + bo_ref[...]
    o_ref[...] = _ln(DN_ALPHA * x + y, g1_ref[...], b1_ref[...])


def _mixer_out(x2, ya2, wu, wvs, wg, bg, lsg, lsb, wsp, bsp, wa, wb, wo, bo, g1, b1, tm):
    tokens = x2.shape[0]
    return pl.pallas_call(
        _mixer_out_kernel,
        grid=(tokens // tm,),
        in_specs=[
            pl.BlockSpec((tm, D_MODEL), lambda i: (i, 0)),
            pl.BlockSpec((tm, SB_WIDTH), lambda i: (i, 0)),
            _resident(wu.shape), _resident(wvs.shape), _resident(wg.shape), _resident(bg.shape),
            _resident(lsg.shape), _resident(lsb.shape), _resident(wsp.shape), _resident(bsp.shape),
            _resident(wa.shape), _resident(wb.shape), _resident(wo.shape), _resident(bo.shape),
            _resident(g1.shape), _resident(b1.shape),
        ],
        out_specs=pl.BlockSpec((tm, D_MODEL), lambda i: (i, 0)),
        out_shape=jax.ShapeDtypeStruct((tokens, D_MODEL), F32),
        compiler_params=pltpu.CompilerParams(
            dimension_semantics=("arbitrary",), vmem_limit_bytes=VMEM_LIMIT),
        name="mixer_out",
    )(x2, ya2, wu, wvs, wg, bg, lsg, lsb, wsp, bsp, wa, wb, wo, bo, g1, b1)


def _conv_ffn_kernel(h_ref, wup_ref, cw_ref, cb_ref, wd_ref, bd_ref, g2_ref, b2_ref, o_ref,
                     hb_ref, slab_ref, tail_ref, acc_ref, *, tiles_per_seq):
    tm = h_ref.shape[0]
    half = FF_SLABS // 2
    hb_ref[...] = h_ref[...].astype(BF16)
    seq_start = (pl.program_id(0) % tiles_per_seq) == 0

    def produce(c, slot):
        for part in range(2):
            c0 = part * D_FF + c * FF_PAIR
            up = _dot(hb_ref[...], wup_ref[:, c0:c0 + FF_PAIR])
            for s in range(half):
                slab_ref[slot, part * half + s, 8:tm + 8, :] = up[:, s * LANES:(s + 1) * LANES]

    def consume(c, slot):
        conv = []
        for s in range(FF_SLABS):
            c0 = (s // half) * D_FF + c * FF_PAIR + (s % half) * LANES
            cw = cw_ref[:, c0:c0 + LANES]
            slab_ref[slot, s, 0:8, :] = jnp.where(seq_start, 0.0, tail_ref[c, s])
            tail_ref[c, s] = slab_ref[slot, s, tm:tm + 8, :]
            conv.append(cb_ref[:, c0:c0 + LANES]
                        + cw[0:1] * slab_ref[slot, s, 6:tm + 6, :]
                        + cw[1:2] * slab_ref[slot, s, 7:tm + 7, :]
                        + cw[2:3] * slab_ref[slot, s, 8:tm + 8, :])
        act = jnp.concatenate([(conv[s] * _sigmoid(conv[s]) * conv[half + s]).astype(BF16)
                               for s in range(half)], axis=1)
        acc_ref[...] += _dot(act, wd_ref[c * FF_PAIR:(c + 1) * FF_PAIR, :])

    acc_ref[...] = jnp.zeros_like(acc_ref)
    produce(0, 0)
    for c in range(FF_STEPS):
        if c + 1 < FF_STEPS:
            produce(c + 1, (c + 1) % 2)
        consume(c, c % 2)
    y = acc_ref[...] + bd_ref[...]
    o_ref[...] = _ln(DN_ALPHA * h_ref[...] + y, g2_ref[...], b2_ref[...])


def _conv_ffn(h2, wup, cw, cb, wd, bd, g2, b2, seq, tm):
    tokens = h2.shape[0]
    return pl.pallas_call(
        functools.partial(_conv_ffn_kernel, tiles_per_seq=seq // tm),
        grid=(tokens // tm,),
        in_specs=[
            pl.BlockSpec((tm, D_MODEL), lambda i: (i, 0)),
            _resident(wup.shape), _resident(cw.shape), _resident(cb.shape), _resident(wd.shape),
            _resident(bd.shape), _resident(g2.shape), _resident(b2.shape),
        ],
        out_specs=pl.BlockSpec((tm, D_MODEL), lambda i: (i, 0)),
        out_shape=jax.ShapeDtypeStruct((tokens, D_MODEL), F32),
        scratch_shapes=[
            pltpu.VMEM((tm, D_MODEL), BF16),
            pltpu.VMEM((2, FF_SLABS, tm + 8, LANES), F32),
            pltpu.VMEM((FF_STEPS, FF_SLABS, 8, LANES), F32),
            pltpu.VMEM((tm, D_MODEL), F32),
        ],
        compiler_params=pltpu.CompilerParams(
            dimension_semantics=("arbitrary",), vmem_limit_bytes=VMEM_LIMIT),
        name="conv_ffn",
    )(h2, wup, cw, cb, wd, bd, g2, b2)


def _tri_const():
    j = jnp.arange(Q_BLOCK)[:, None]
    s = jnp.arange(Q_BLOCK)[None, :]
    half = jnp.concatenate([-(j >= s).astype(F32), -jnp.ones((Q_BLOCK, Q_BLOCK), F32)], axis=1)
    return jnp.concatenate([half, half], axis=0).astype(BF16)


def _layer(h, w_in, b_gate, ln_sg_g, ln_sg_b, w_spatial, b_spatial, w_branch_a, w_branch_b, w_out,
           b_out, ln1_g, ln1_b, w_up, conv_w, conv_b, w_down, b_down, ln2_g, ln2_b):
    batch, seq, _ = h.shape
    tokens = batch * seq
    x2 = h.reshape(tokens, D_MODEL)
    row = lambda v: v.reshape(1, -1)

    wq = w_in[:, 0:SB_WIDTH].astype(BF16)
    wkt = w_in[:, SB_WIDTH:2 * SB_WIDTH].T.astype(BF16)
    wv = w_in[:, 2 * SB_WIDTH:3 * SB_WIDTH].astype(BF16)
    wu = w_in[:, 3 * SB_WIDTH:3 * SB_WIDTH + SG_WIDTH].astype(BF16)
    wvs = w_in[:, 3 * SB_WIDTH + SG_WIDTH:3 * SB_WIDTH + 2 * SG_WIDTH].astype(BF16)
    wg = w_in[:, 3 * SB_WIDTH + 2 * SG_WIDTH:].astype(BF16)
    bsp = jnp.repeat(b_spatial.T, SG_WIDTH // N_SG_GROUPS, axis=1)

    q2, kt3, v2 = _qkv_proj(x2, wq, wkt, wv, batch, seq, tm=512)
    ya3 = _sb_attention(q2.reshape(batch, seq, SB_WIDTH), kt3, v2.reshape(batch, seq, SB_WIDTH),
                        _tri_const(), sub=4)
    h1 = _mixer_out(x2, ya3.reshape(tokens, SB_WIDTH), wu, wvs, wg, row(b_gate), row(ln_sg_g),
                    row(ln_sg_b), w_spatial, bsp, w_branch_a.astype(BF16), w_branch_b.astype(BF16),
                    w_out.astype(BF16), row(b_out), row(ln1_g), row(ln1_b), tm=512)
    h2 = _conv_ffn(h1, w_up.astype(BF16), conv_w, row(conv_b), w_down.astype(BF16), row(b_down),
                   row(ln2_g), row(ln2_b), seq, tm=512)
    return h2.reshape(batch, seq, D_MODEL)


def kernel(x, w_in, b_gate, ln_sg_g, ln_sg_b, w_spatial, b_spatial, w_branch_a, w_branch_b, w_out,
           b_out, ln1_g, ln1_b, w_up, conv_w, conv_b, w_down, b_down, ln2_g, ln2_b):
    h = x
    for l in range(w_in.shape[0]):
        h = _layer(h, w_in[l], b_gate[l], ln_sg_g[l], ln_sg_b[l], w_spatial[l], b_spatial[l],
                   w_branch_a[l], w_branch_b[l], w_out[l], b_out[l], ln1_g[l], ln1_b[l], w_up[l],
                   conv_w[l], conv_b[l], w_down[l], b_down[l], ln2_g[l], ln2_b[l])
    return h
```

```python
import functools

import jax
import jax.numpy as jnp
from jax import lax
from jax.experimental import pallas as pl
from jax.experimental.pallas import tpu as pltpu

F32 = jnp.float32
BF16 = jnp.bfloat16

D_MODEL = 1024
N_HEADS = 8
HEAD_DIM = 64
SB_WIDTH = N_HEADS * HEAD_DIM
SG_WIDTH = 512
COL_U = 3 * SB_WIDTH
COL_VSG = COL_U + SG_WIDTH
COL_GATE = COL_VSG + SG_WIDTH
N_SG_GROUPS = 8
CHUNK = 128
Q_BLOCK = 128
D_FF = 2816
LN_EPS = 1e-5
LOG2E = 1.4426950408889634
STICK_CUTOFF = 64.0 * LOG2E
DN_ALPHA = 2.0 ** 0.25

LANES = 128
PAIR = 2 * HEAD_DIM
N_PAIRS = N_HEADS // 2
FF_PAIR = 256
FF_STEPS = D_FF // FF_PAIR
FF_SLABS = 2 * FF_PAIR // LANES
FFN_TM = 512
VMEM_LIMIT = 56 * 1024 * 1024


def _ln(x, g, b):
    mu = jnp.mean(x, axis=-1, keepdims=True)
    xc = x - mu
    var = jnp.mean(xc * xc, axis=-1, keepdims=True)
    return xc * lax.rsqrt(var + LN_EPS) * g + b


def _gelu_tanh(x):
    c = 0.7978845608028654
    return x * (0.5 * (1.0 + jnp.tanh(c * (x + 0.044715 * (x * x * x)))))


def _sigmoid(x):
    return 1.0 / (1.0 + jnp.exp(-x))


def _dot(a, b):
    return jnp.dot(a, b, preferred_element_type=F32)


def _resident(shape):
    nd = len(shape)
    return pl.BlockSpec(shape, lambda *_: (0,) * nd, pipeline_mode=pl.Buffered(1))


def _qkv_kernel(x_ref, wq_ref, wkt_ref, wv_ref, q_ref, kt_ref, v_ref):
    x = x_ref[...]
    q_ref[...] = (_dot(x, wq_ref[...]) * (0.125 * LOG2E)).astype(BF16)
    v_ref[...] = _dot(x, wv_ref[...]).astype(BF16)
    kt = lax.dot_general(wkt_ref[...], x, (((1,), (1,)), ((), ())),
                         preferred_element_type=F32)
    kt_ref[0] = kt.astype(BF16)


def _qkv_proj(x2, w_in, wkt, batch, seq, tm):
    tokens = x2.shape[0]
    tiles_per_seq = seq // tm
    col_block = lambda k: pl.BlockSpec((D_MODEL, SB_WIDTH), lambda i: (0, k), pipeline_mode=pl.Buffered(1))
    return pl.pallas_call(
        _qkv_kernel,
        grid=(tokens // tm,),
        in_specs=[
            pl.BlockSpec((tm, D_MODEL), lambda i: (i, 0)),
            col_block(0),
            _resident((SB_WIDTH, D_MODEL)),
            col_block(2),
        ],
        out_specs=[
            pl.BlockSpec((tm, SB_WIDTH), lambda i: (i, 0)),
            pl.BlockSpec((1, SB_WIDTH, tm), lambda i: (i // tiles_per_seq, 0, i % tiles_per_seq)),
            pl.BlockSpec((tm, SB_WIDTH), lambda i: (i, 0)),
        ],
        out_shape=[
            jax.ShapeDtypeStruct((tokens, SB_WIDTH), BF16),
            jax.ShapeDtypeStruct((batch, SB_WIDTH, seq), BF16),
            jax.ShapeDtypeStruct((tokens, SB_WIDTH), BF16),
        ],
        compiler_params=pltpu.CompilerParams(
            dimension_semantics=("arbitrary",), vmem_limit_bytes=VMEM_LIMIT),
        name="qkv_proj",
    )(x2, w_in, wkt, w_in)


def _softplus2(z):
    return jnp.maximum(z, 0.0) + jnp.log(1.0 + jnp.exp2(-jnp.abs(z))) * LOG2E


def _sb_attn_kernel(q_ref, kt_ref, v_ref, tri_ref, o_ref, kbd_ref, vbd_ref, pre_ref, rs_ref,
                    carry_ref, acc_ref, *, n_kblocks, sub, npair):
    qs = pl.program_id(2)
    mq = sub * Q_BLOCK
    pairs = range(npair)

    @pl.when(qs == 0)
    def _():
        row_lo = lax.broadcasted_iota(jnp.int32, (PAIR, Q_BLOCK), 0) < HEAD_DIM
        lane_lo = lax.broadcasted_iota(jnp.int32, (Q_BLOCK, PAIR), 1) < HEAD_DIM
        zero = jnp.zeros((PAIR, Q_BLOCK), BF16)
        for p in pairs:
            for j in range(n_kblocks):
                kt = kt_ref[0, p * PAIR:(p + 1) * PAIR, j * Q_BLOCK:(j + 1) * Q_BLOCK]
                kbd_ref[p, j, :, 0:Q_BLOCK] = jnp.where(row_lo, kt, zero)
                kbd_ref[p, j, :, Q_BLOCK:2 * Q_BLOCK] = jnp.where(row_lo, zero, kt)
                vt = v_ref[0, j * Q_BLOCK:(j + 1) * Q_BLOCK, p * PAIR:(p + 1) * PAIR]
                vbd_ref[p, j, 0:Q_BLOCK, :] = jnp.where(lane_lo, vt, zero)
                vbd_ref[p, j, Q_BLOCK:2 * Q_BLOCK, :] = jnp.where(lane_lo, zero, vt)
            vbd_ref[p, n_kblocks] = jnp.zeros((2 * Q_BLOCK, PAIR), BF16)

    tri = tri_ref[...]
    row = lax.broadcasted_iota(jnp.int32, (Q_BLOCK, 2 * Q_BLOCK), 0)
    col = lax.broadcasted_iota(jnp.int32, (Q_BLOCK, 2 * Q_BLOCK), 1) & (Q_BLOCK - 1)
    causal = col < row

    def mask_top(x, fill, diag):
        if not diag:
            return x
        top = jnp.where(causal, x[:Q_BLOCK], fill)
        return top if x.shape[0] == Q_BLOCK else jnp.concatenate([top, x[Q_BLOCK:]], axis=0)

    def stage1(p, r0, j, diag):
        q2 = q_ref[0, r0:mq, p * PAIR:(p + 1) * PAIR]
        z = _dot(q2, kbd_ref[p, j])
        sp = mask_top(_softplus2(z), 0.0, diag)
        hi = sp.astype(BF16)
        lo = (sp - hi.astype(F32)).astype(BF16)
        cr = [_dot(jnp.concatenate([hi[:, h * Q_BLOCK:(h + 1) * Q_BLOCK],
                                    lo[:, h * Q_BLOCK:(h + 1) * Q_BLOCK]], axis=1), tri)
              for h in range(2)]
        pre = z + jnp.concatenate([cr[0][:, :Q_BLOCK], cr[1][:, :Q_BLOCK]], axis=1)
        rs = jnp.concatenate([cr[0][:, Q_BLOCK:], cr[1][:, Q_BLOCK:]], axis=1)
        return pre, rs

    def stage2(p, pre, rs, r0, j, diag, jv=None):
        c = carry_ref[p, r0:mq, :]
        c_new = c + rs
        carry_ref[p, r0:mq, :] = c_new
        w = jnp.exp2(mask_top(pre + c, -1e30, diag)).astype(BF16)
        acc_ref[p, r0:mq, :] += _dot(w, vbd_ref[p, j if jv is None else jv])
        return jnp.max(c_new)

    carry_ref[...] = jnp.zeros_like(carry_ref)
    acc_ref[...] = jnp.zeros_like(acc_ref)
    j0 = qs * sub
    for d in reversed(range(sub)):
        r0 = d * Q_BLOCK
        for p in pairs:
            pre, rs = stage1(p, r0, j0 + d, True)
            stage2(p, pre, rs, r0, j0 + d, True)
    j1 = jnp.maximum(j0 - 1, 0)
    jv = jnp.where(qs > 0, j1, n_kblocks)
    stick = []
    for p in pairs:
        pre, rs = stage1(p, 0, j1, False)
        stick.append(stage2(p, pre, rs, 0, j1, False, jv=jv))

    alive0 = functools.reduce(jnp.maximum, stick) >= -STICK_CUTOFF

    @pl.when(jnp.logical_and(j0 >= 2, alive0))
    def _():
        for p in pairs:
            pre, rs = stage1(p, 0, j0 - 2, False)
            pre_ref[p] = pre
            rs_ref[p] = rs

        def cond(state):
            j, alive = state
            return jnp.logical_and(j >= 1, alive > 0)

        def body(state):
            j, _ = state
            stick = []
            for p in pairs:
                stick.append(stage2(p, pre_ref[p], rs_ref[p], 0, j, False))
                pre_n, rs_n = stage1(p, 0, j - 1, False)
                pre_ref[p] = pre_n
                rs_ref[p] = rs_n
            alive = functools.reduce(jnp.maximum, stick) >= -STICK_CUTOFF
            return j - 1, alive.astype(jnp.int32)

        j_end, alive = lax.while_loop(cond, body, (j0 - 2, jnp.int32(1)))

        @pl.when(jnp.logical_and(j_end == 0, alive > 0))
        def _():
            for p in pairs:
                stage2(p, pre_ref[p], rs_ref[p], 0, 0, False)

    for p in pairs:
        o_ref[0, :, p * PAIR:(p + 1) * PAIR] = acc_ref[p].astype(BF16)


def _sb_attention(q3, kt3, v3, tri, sub, npair):
    batch, seq, _ = q3.shape
    n_blocks = seq // Q_BLOCK
    mq = sub * Q_BLOCK
    width = npair * PAIR
    return pl.pallas_call(
        functools.partial(_sb_attn_kernel, n_kblocks=n_blocks, sub=sub, npair=npair),
        grid=(batch, N_PAIRS // npair, seq // mq),
        in_specs=[
            pl.BlockSpec((1, mq, width), lambda b, g, i: (b, i, g)),
            pl.BlockSpec((1, width, seq), lambda b, g, i: (b, g, 0)),
            pl.BlockSpec((1, seq, width), lambda b, g, i: (b, 0, g)),
            _resident((2 * Q_BLOCK, 2 * Q_BLOCK)),
        ],
        out_specs=pl.BlockSpec((1, mq, width), lambda b, g, i: (b, i, g)),
        out_shape=jax.ShapeDtypeStruct((batch, seq, SB_WIDTH), BF16),
        scratch_shapes=[
            pltpu.VMEM((npair, n_blocks, PAIR, 2 * Q_BLOCK), BF16),
            pltpu.VMEM((npair, n_blocks + 1, 2 * Q_BLOCK, PAIR), BF16),
            pltpu.VMEM((npair, mq, 2 * Q_BLOCK), F32),
            pltpu.VMEM((npair, mq, 2 * Q_BLOCK), F32),
            pltpu.VMEM((npair, mq, 2 * Q_BLOCK), F32),
            pltpu.VMEM((npair, mq, PAIR), F32),
        ],
        compiler_params=pltpu.CompilerParams(
            dimension_semantics=("arbitrary", "arbitrary", "arbitrary"),
            vmem_limit_bytes=VMEM_LIMIT),
        name="sb_attn",
    )(q3, kt3, v3, tri)


def _mixer_out_kernel(x_ref, ya_ref, win_ref, bg_ref, lsg_ref, lsb_ref, wsp_ref,
                      bsp_ref, wa_ref, wb_ref, wo_ref, bo_ref, g1_ref, b1_ref, o_ref):
    x = x_ref[...]
    tm = x.shape[0]

    u = _gelu_tanh(_dot(x, win_ref[:, COL_U:COL_VSG]))
    vs = _ln(_gelu_tanh(_dot(x, win_ref[:, COL_VSG:COL_GATE])), lsg_ref[...], lsb_ref[...]).astype(BF16)

    tril = (lax.broadcasted_iota(jnp.int32, (CHUNK, CHUNK), 1)
            <= lax.broadcasted_iota(jnp.int32, (CHUNK, CHUNK), 0))
    wsp = [jnp.where(tril, wsp_ref[g], 0.0).astype(BF16) for g in range(N_SG_GROUPS)]
    lane_lo = lax.broadcasted_iota(jnp.int32, (CHUNK, LANES), 1) < HEAD_DIM
    zero = jnp.zeros((CHUNK, LANES), BF16)
    rows = []
    for c in range(tm // CHUNK):
        parts = []
        for p in range(N_SG_GROUPS // 2):
            v2 = vs[c * CHUNK:(c + 1) * CHUNK, p * LANES:(p + 1) * LANES]
            parts.append(_dot(wsp[2 * p], jnp.where(lane_lo, v2, zero))
                         + _dot(wsp[2 * p + 1], jnp.where(lane_lo, zero, v2)))
        rows.append(jnp.concatenate(parts, axis=1) + bsp_ref[...])
    yb = u * jnp.concatenate(rows, axis=0)

    g = _sigmoid(_dot(x, win_ref[:, COL_GATE:]) + bg_ref[...])
    merged = (g[:, :D_MODEL] * _dot(ya_ref[...], wa_ref[...])
              + g[:, D_MODEL:] * _dot(yb, wb_ref[...]))
    y = _dot(merged, wo_ref[...]) + bo_ref[...]
    o_ref[...] = _ln(DN_ALPHA * x + y, g1_ref[...], b1_ref[...])


def _mixer_out(x2, ya2, w_in, bg, lsg, lsb, wsp, bsp, wa, wb, wo, bo, g1, b1, tm):
    tokens = x2.shape[0]
    return pl.pallas_call(
        _mixer_out_kernel,
        grid=(tokens // tm,),
        in_specs=[
            pl.BlockSpec((tm, D_MODEL), lambda i: (i, 0)),
            pl.BlockSpec((tm, SB_WIDTH), lambda i: (i, 0)),
            _resident(w_in.shape), _resident(bg.shape),
            _resident(lsg.shape), _resident(lsb.shape), _resident(wsp.shape), _resident(bsp.shape),
            _resident(wa.shape), _resident(wb.shape), _resident(wo.shape), _resident(bo.shape),
            _resident(g1.shape), _resident(b1.shape),
        ],
        out_specs=pl.BlockSpec((tm, D_MODEL), lambda i: (i, 0)),
        out_shape=jax.ShapeDtypeStruct((tokens, D_MODEL), F32),
        compiler_params=pltpu.CompilerParams(
            dimension_semantics=("arbitrary",), vmem_limit_bytes=VMEM_LIMIT),
        name="mixer_out",
    )(x2, ya2, w_in, bg, lsg, lsb, wsp, bsp, wa, wb, wo, bo, g1, b1)


def _conv_ffn_kernel(h_ref, wup_ref, cw_ref, cb_ref, wd_ref, bd_ref, g2_ref, b2_ref, o_ref,
                     slab_ref, tail_ref, acc_ref, *, tiles_per_seq):
    tm = h_ref.shape[0]
    half = FF_SLABS // 2
    seq_start = (pl.program_id(0) % tiles_per_seq) == 0

    def produce(c, slot):
        for part in range(2):
            c0 = part * D_FF + c * FF_PAIR
            up = _dot(h_ref[...], wup_ref[:, c0:c0 + FF_PAIR])
            for s in range(half):
                slab_ref[slot, part * half + s, 8:tm + 8, :] = up[:, s * LANES:(s + 1) * LANES]

    def consume(c, slot):
        conv = []
        for s in range(FF_SLABS):
            c0 = (s // half) * D_FF + c * FF_PAIR + (s % half) * LANES
            cw = cw_ref[:, c0:c0 + LANES]
            slab_ref[slot, s, 0:8, :] = jnp.where(seq_start, 0.0, tail_ref[c, s])
            tail_ref[c, s] = slab_ref[slot, s, tm:tm + 8, :]
            conv.append(cb_ref[:, c0:c0 + LANES]
                        + cw[0:1] * slab_ref[slot, s, 6:tm + 6, :]
                        + cw[1:2] * slab_ref[slot, s, 7:tm + 7, :]
                        + cw[2:3] * slab_ref[slot, s, 8:tm + 8, :])
        act = jnp.concatenate([conv[s] * _sigmoid(conv[s]) * conv[half + s]
                               for s in range(half)], axis=1)
        acc_ref[...] += _dot(act, wd_ref[c * FF_PAIR:(c + 1) * FF_PAIR, :])

    acc_ref[...] = DN_ALPHA * h_ref[...] + bd_ref[...]
    produce(0, 0)
    for c in range(FF_STEPS):
        if c + 1 < FF_STEPS:
            produce(c + 1, (c + 1) % 2)
        consume(c, c % 2)
    o_ref[...] = _ln(acc_ref[...], g2_ref[...], b2_ref[...])


def _conv_ffn(h2, wup, cw, cb, wd, bd, g2, b2, seq, tm):
    tokens = h2.shape[0]
    return pl.pallas_call(
        functools.partial(_conv_ffn_kernel, tiles_per_seq=seq // tm),
        grid=(tokens // tm,),
        in_specs=[
            pl.BlockSpec((tm, D_MODEL), lambda i: (i, 0)),
            _resident(wup.shape), _resident(cw.shape), _resident(cb.shape), _resident(wd.shape),
            _resident(bd.shape), _resident(g2.shape), _resident(b2.shape),
        ],
        out_specs=pl.BlockSpec((tm, D_MODEL), lambda i: (i, 0)),
        out_shape=jax.ShapeDtypeStruct((tokens, D_MODEL), F32),
        scratch_shapes=[
            pltpu.VMEM((2, FF_SLABS, tm + 8, LANES), F32),
            pltpu.VMEM((FF_STEPS, FF_SLABS, 8, LANES), F32),
            pltpu.VMEM((tm, D_MODEL), F32),
        ],
        compiler_params=pltpu.CompilerParams(
            dimension_semantics=("arbitrary",), vmem_limit_bytes=VMEM_LIMIT),
        name="conv_ffn",
    )(h2, wup, cw, cb, wd, bd, g2, b2)


def _tri_const():
    j = jnp.arange(Q_BLOCK)[:, None]
    s = jnp.arange(Q_BLOCK)[None, :]
    half = jnp.concatenate([-(j >= s).astype(F32), -jnp.ones((Q_BLOCK, Q_BLOCK), F32)], axis=1)
    return jnp.concatenate([half, half], axis=0).astype(BF16)


def _layer(h, w_in, b_gate, ln_sg_g, ln_sg_b, w_spatial, b_spatial, w_branch_a, w_branch_b, w_out,
           b_out, ln1_g, ln1_b, w_up, conv_w, conv_b, w_down, b_down, ln2_g, ln2_b):
    batch, seq, _ = h.shape
    tokens = batch * seq
    x2 = h.reshape(tokens, D_MODEL)
    row = lambda v: v.reshape(1, -1)

    wkt = w_in[:, SB_WIDTH:2 * SB_WIDTH].T
    bsp = jnp.repeat(b_spatial.T, SG_WIDTH // N_SG_GROUPS, axis=1)

    q2, kt3, v2 = _qkv_proj(x2, w_in, wkt, batch, seq, tm=1024)
    ya3 = _sb_attention(q2.reshape(batch, seq, SB_WIDTH), kt3, v2.reshape(batch, seq, SB_WIDTH),
                        _tri_const(), sub=4, npair=2)
    h1 = _mixer_out(x2, ya3.reshape(tokens, SB_WIDTH), w_in, row(b_gate), row(ln_sg_g),
                    row(ln_sg_b), w_spatial, bsp, w_branch_a.astype(BF16), w_branch_b, w_out,
                    row(b_out), row(ln1_g), row(ln1_b), tm=512)
    h2 = _conv_ffn(h1, w_up, conv_w, row(conv_b), w_down, row(b_down), row(ln2_g), row(ln2_b), seq,
                   tm=FFN_TM)
    return h2.reshape(batch, seq, D_MODEL)


def kernel(x, w_in, b_gate, ln_sg_g, ln_sg_b, w_spatial, b_spatial, w_branch_a, w_branch_b, w_out,
           b_out, ln1_g, ln1_b, w_up, conv_w, conv_b, w_down, b_down, ln2_g, ln2_b):
    h = x
    for l in range(w_in.shape[0]):
        h = _layer(h, w_in[l], b_gate[l], ln_sg_g[l], ln_sg_b[l], w_spatial[l], b_spatial[l],
                   w_branch_a[l], w_branch_b[l], w_out[l], b_out[l], ln1_g[l], ln1_b[l], w_up[l],
                   conv_w[l], conv_b[l], w_down[l], b_down[l], ln2_g[l], ln2_b[l])
    return h
```

```python
import functools

import jax
import jax.numpy as jnp
from jax import lax
from jax.experimental import pallas as pl
from jax.experimental.pallas import tpu as pltpu

F32 = jnp.float32
BF16 = jnp.bfloat16

D_MODEL = 1024
N_HEADS = 8
HEAD_DIM = 64
SB_WIDTH = N_HEADS * HEAD_DIM
SG_WIDTH = 512
COL_U = 3 * SB_WIDTH
COL_VSG = COL_U + SG_WIDTH
COL_GATE = COL_VSG + SG_WIDTH
N_SG_GROUPS = 8
CHUNK = 128
Q_BLOCK = 128
D_FF = 2816
LN_EPS = 1e-5
LOG2E = 1.4426950408889634
STICK_CUTOFF = 64.0 * LOG2E
DN_ALPHA = 2.0 ** 0.25

LANES = 128
PAIR = 2 * HEAD_DIM
N_PAIRS = N_HEADS // 2
FF_PAIR = 256
FF_STEPS = D_FF // FF_PAIR
FF_SLABS = 2 * FF_PAIR // LANES
FFN_TM = 512
VMEM_LIMIT = 56 * 1024 * 1024


def _ln(x, g, b):
    mu = jnp.mean(x, axis=-1, keepdims=True)
    xc = x - mu
    var = jnp.mean(xc * xc, axis=-1, keepdims=True)
    return xc * lax.rsqrt(var + LN_EPS) * g + b


def _gelu_tanh(x):
    c = 0.7978845608028654
    return x * (0.5 * (1.0 + jnp.tanh(c * (x + 0.044715 * (x * x * x)))))


def _sigmoid(x):
    return 1.0 / (1.0 + jnp.exp(-x))


def _dot(a, b):
    return jnp.dot(a, b, preferred_element_type=F32)


def _resident(shape):
    nd = len(shape)
    return pl.BlockSpec(shape, lambda *_: (0,) * nd, pipeline_mode=pl.Buffered(1))


def _qkv_kernel(x_ref, wq_ref, wkt_ref, wv_ref, q_ref, kt_ref, v_ref):
    x = x_ref[...]
    q_ref[...] = (_dot(x, wq_ref[...]) * (0.125 * LOG2E)).astype(BF16)
    v_ref[...] = _dot(x, wv_ref[...]).astype(BF16)
    kt = lax.dot_general(wkt_ref[...], x, (((1,), (1,)), ((), ())),
                         preferred_element_type=F32)
    kt_ref[0] = kt.astype(BF16)


def _qkv_proj(x2, w_in, wkt, batch, seq, tm):
    tokens = x2.shape[0]
    tiles_per_seq = seq // tm
    col_block = lambda k: pl.BlockSpec((D_MODEL, SB_WIDTH), lambda i: (0, k), pipeline_mode=pl.Buffered(1))
    return pl.pallas_call(
        _qkv_kernel,
        grid=(tokens // tm,),
        in_specs=[
            pl.BlockSpec((tm, D_MODEL), lambda i: (i, 0)),
            col_block(0),
            _resident((SB_WIDTH, D_MODEL)),
            col_block(2),
        ],
        out_specs=[
            pl.BlockSpec((tm, SB_WIDTH), lambda i: (i, 0)),
            pl.BlockSpec((1, SB_WIDTH, tm), lambda i: (i // tiles_per_seq, 0, i % tiles_per_seq)),
            pl.BlockSpec((tm, SB_WIDTH), lambda i: (i, 0)),
        ],
        out_shape=[
            jax.ShapeDtypeStruct((tokens, SB_WIDTH), BF16),
            jax.ShapeDtypeStruct((batch, SB_WIDTH, seq), BF16),
            jax.ShapeDtypeStruct((tokens, SB_WIDTH), BF16),
        ],
        compiler_params=pltpu.CompilerParams(
            dimension_semantics=("arbitrary",), vmem_limit_bytes=VMEM_LIMIT),
        name="qkv_proj",
    )(x2, w_in, wkt, w_in)


def _softplus2(z):
    return jnp.maximum(z, 0.0) + jnp.log(1.0 + jnp.exp2(-jnp.abs(z))) * LOG2E


def _sb_attn_kernel(q_ref, kt_ref, v_ref, tri_ref, o_ref, kbd_ref, vbd_ref, pre_ref, rs_ref,
                    carry_ref, acc_ref, *, n_kblocks, sub, npair):
    qs = pl.program_id(2)
    mq = sub * Q_BLOCK
    pairs = range(npair)

    @pl.when(qs == 0)
    def _():
        row_lo = lax.broadcasted_iota(jnp.int32, (PAIR, Q_BLOCK), 0) < HEAD_DIM
        lane_lo = lax.broadcasted_iota(jnp.int32, (Q_BLOCK, PAIR), 1) < HEAD_DIM
        zero = jnp.zeros((PAIR, Q_BLOCK), BF16)
        for p in pairs:
            for j in range(n_kblocks):
                kt = kt_ref[0, p * PAIR:(p + 1) * PAIR, j * Q_BLOCK:(j + 1) * Q_BLOCK]
                kbd_ref[p, j, :, 0:Q_BLOCK] = jnp.where(row_lo, kt, zero)
                kbd_ref[p, j, :, Q_BLOCK:2 * Q_BLOCK] = jnp.where(row_lo, zero, kt)
                vt = v_ref[0, j * Q_BLOCK:(j + 1) * Q_BLOCK, p * PAIR:(p + 1) * PAIR]
                vbd_ref[p, j, 0:Q_BLOCK, :] = jnp.where(lane_lo, vt, zero)
                vbd_ref[p, j, Q_BLOCK:2 * Q_BLOCK, :] = jnp.where(lane_lo, zero, vt)
            vbd_ref[p, n_kblocks] = jnp.zeros((2 * Q_BLOCK, PAIR), BF16)

    tri = tri_ref[...]
    row = lax.broadcasted_iota(jnp.int32, (Q_BLOCK, 2 * Q_BLOCK), 0)
    col = lax.broadcasted_iota(jnp.int32, (Q_BLOCK, 2 * Q_BLOCK), 1) & (Q_BLOCK - 1)
    causal = col < row

    def mask_top(x, fill, diag):
        if not diag:
            return x
        top = jnp.where(causal, x[:Q_BLOCK], fill)
        return top if x.shape[0] == Q_BLOCK else jnp.concatenate([top, x[Q_BLOCK:]], axis=0)

    def stage1(p, r0, r1, j, diag):
        q2 = q_ref[0, r0:r1, p * PAIR:(p + 1) * PAIR]
        z = _dot(q2, kbd_ref[p, j])
        sp = mask_top(_softplus2(z), 0.0, diag)
        hi = sp.astype(BF16)
        lo = (sp - hi.astype(F32)).astype(BF16)
        cr = [_dot(jnp.concatenate([hi[:, h * Q_BLOCK:(h + 1) * Q_BLOCK],
                                    lo[:, h * Q_BLOCK:(h + 1) * Q_BLOCK]], axis=1), tri)
              for h in range(2)]
        pre = z + jnp.concatenate([cr[0][:, :Q_BLOCK], cr[1][:, :Q_BLOCK]], axis=1)
        rs = jnp.concatenate([cr[0][:, Q_BLOCK:], cr[1][:, Q_BLOCK:]], axis=1)
        return pre, rs

    def stage2(p, pre, rs, r0, r1, j, diag, jv=None):
        c = carry_ref[p, r0:r1, :]
        c_new = c + rs
        carry_ref[p, r0:r1, :] = c_new
        w = jnp.exp2(mask_top(pre + c, -1e30, diag)).astype(BF16)
        acc_ref[p, r0:r1, :] += _dot(w, vbd_ref[p, j if jv is None else jv])
        return jnp.max(c_new[-Q_BLOCK:])

    def block(p, r0, r1, j, diag, jv=None):
        pre, rs = stage1(p, r0, r1, j, diag)
        return stage2(p, pre, rs, r0, r1, j, diag, jv)

    def stick_left():
        return functools.reduce(jnp.maximum, [jnp.max(carry_ref[p]) for p in pairs])

    carry_ref[...] = jnp.zeros_like(carry_ref)
    acc_ref[...] = jnp.zeros_like(acc_ref)
    j0 = qs * sub
    j1 = jnp.maximum(j0 - 1, 0)
    jv = jnp.where(qs > 0, j1, n_kblocks)

    stick = []
    for d in reversed(range(-1, sub)):
        r0, r1 = max(d, 0) * Q_BLOCK, min(d + 2, sub) * Q_BLOCK
        for p in pairs:
            left = block(p, r0, r1, j0 + d, True) if d >= 0 else block(p, r0, r1, j1, False, jv)
            if d < sub - 1:
                stick.append(left)

    @pl.when(functools.reduce(jnp.maximum, stick) >= -STICK_CUTOFF)
    def _():
        for d in reversed(range(0, sub - 2)):
            for p in pairs:
                block(p, (d + 2) * Q_BLOCK, mq, j0 + d, False)

        @pl.when(qs > 0)
        def _():
            for p in pairs:
                block(p, Q_BLOCK, mq, j0 - 1, False)

            @pl.when(jnp.logical_and(j0 >= 2, stick_left() >= -STICK_CUTOFF))
            def _():
                for p in pairs:
                    pre, rs = stage1(p, 0, mq, j0 - 2, False)
                    pre_ref[p] = pre
                    rs_ref[p] = rs

                def cond(state):
                    j, alive = state
                    return jnp.logical_and(j >= 1, alive > 0)

                def body(state):
                    j, _ = state
                    for p in pairs:
                        stage2(p, pre_ref[p], rs_ref[p], 0, mq, j, False)
                        pre_n, rs_n = stage1(p, 0, mq, j - 1, False)
                        pre_ref[p] = pre_n
                        rs_ref[p] = rs_n
                    return j - 1, (stick_left() >= -STICK_CUTOFF).astype(jnp.int32)

                j_end, alive = lax.while_loop(cond, body, (j0 - 2, jnp.int32(1)))

                @pl.when(jnp.logical_and(j_end == 0, alive > 0))
                def _():
                    for p in pairs:
                        stage2(p, pre_ref[p], rs_ref[p], 0, mq, 0, False)

    for p in pairs:
        o_ref[0, :, p * PAIR:(p + 1) * PAIR] = acc_ref[p].astype(BF16)


def _sb_attention(q3, kt3, v3, tri, sub, npair):
    batch, seq, _ = q3.shape
    n_blocks = seq // Q_BLOCK
    mq = sub * Q_BLOCK
    width = npair * PAIR
    return pl.pallas_call(
        functools.partial(_sb_attn_kernel, n_kblocks=n_blocks, sub=sub, npair=npair),
        grid=(batch, N_PAIRS // npair, seq // mq),
        in_specs=[
            pl.BlockSpec((1, mq, width), lambda b, g, i: (b, i, g)),
            pl.BlockSpec((1, width, seq), lambda b, g, i: (b, g, 0)),
            pl.BlockSpec((1, seq, width), lambda b, g, i: (b, 0, g)),
            _resident((2 * Q_BLOCK, 2 * Q_BLOCK)),
        ],
        out_specs=pl.BlockSpec((1, mq, width), lambda b, g, i: (b, i, g)),
        out_shape=jax.ShapeDtypeStruct((batch, seq, SB_WIDTH), BF16),
        scratch_shapes=[
            pltpu.VMEM((npair, n_blocks, PAIR, 2 * Q_BLOCK), BF16),
            pltpu.VMEM((npair, n_blocks + 1, 2 * Q_BLOCK, PAIR), BF16),
            pltpu.VMEM((npair, mq, 2 * Q_BLOCK), F32),
            pltpu.VMEM((npair, mq, 2 * Q_BLOCK), F32),
            pltpu.VMEM((npair, mq, 2 * Q_BLOCK), F32),
            pltpu.VMEM((npair, mq, PAIR), F32),
        ],
        compiler_params=pltpu.CompilerParams(
            dimension_semantics=("arbitrary", "arbitrary", "arbitrary"),
            vmem_limit_bytes=VMEM_LIMIT),
        name="sb_attn",
    )(q3, kt3, v3, tri)


def _mixer_out_kernel(x_ref, ya_ref, win_ref, bg_ref, lsg_ref, lsb_ref, wsp_ref,
                      bsp_ref, wa_ref, wb_ref, wo_ref, bo_ref, g1_ref, b1_ref, o_ref):
    x = x_ref[...]
    tm = x.shape[0]

    u = _gelu_tanh(_dot(x, win_ref[:, COL_U:COL_VSG]))
    vs = _ln(_gelu_tanh(_dot(x, win_ref[:, COL_VSG:COL_GATE])), lsg_ref[...], lsb_ref[...]).astype(BF16)

    tril = (lax.broadcasted_iota(jnp.int32, (CHUNK, CHUNK), 1)
            <= lax.broadcasted_iota(jnp.int32, (CHUNK, CHUNK), 0))
    wsp = [jnp.where(tril, wsp_ref[g], 0.0).astype(BF16) for g in range(N_SG_GROUPS)]
    lane_lo = lax.broadcasted_iota(jnp.int32, (CHUNK, LANES), 1) < HEAD_DIM
    zero = jnp.zeros((CHUNK, LANES), BF16)
    rows = []
    for c in range(tm // CHUNK):
        parts = []
        for p in range(N_SG_GROUPS // 2):
            v2 = vs[c * CHUNK:(c + 1) * CHUNK, p * LANES:(p + 1) * LANES]
            parts.append(_dot(wsp[2 * p], jnp.where(lane_lo, v2, zero))
                         + _dot(wsp[2 * p + 1], jnp.where(lane_lo, zero, v2)))
        rows.append(jnp.concatenate(parts, axis=1) + bsp_ref[...])
    yb = u * jnp.concatenate(rows, axis=0)

    g = _sigmoid(_dot(x, win_ref[:, COL_GATE:]) + bg_ref[...])
    merged = (g[:, :D_MODEL] * _dot(ya_ref[...], wa_ref[...])
              + g[:, D_MODEL:] * _dot(yb, wb_ref[...]))
    y = _dot(merged, wo_ref[...]) + bo_ref[...]
    o_ref[...] = _ln(DN_ALPHA * x + y, g1_ref[...], b1_ref[...])


def _mixer_out(x2, ya2, w_in, bg, lsg, lsb, wsp, bsp, wa, wb, wo, bo, g1, b1, tm):
    tokens = x2.shape[0]
    return pl.pallas_call(
        _mixer_out_kernel,
        grid=(tokens // tm,),
        in_specs=[
            pl.BlockSpec((tm, D_MODEL), lambda i: (i, 0)),
            pl.BlockSpec((tm, SB_WIDTH), lambda i: (i, 0)),
            _resident(w_in.shape), _resident(bg.shape),
            _resident(lsg.shape), _resident(lsb.shape), _resident(wsp.shape), _resident(bsp.shape),
            _resident(wa.shape), _resident(wb.shape), _resident(wo.shape), _resident(bo.shape),
            _resident(g1.shape), _resident(b1.shape),
        ],
        out_specs=pl.BlockSpec((tm, D_MODEL), lambda i: (i, 0)),
        out_shape=jax.ShapeDtypeStruct((tokens, D_MODEL), F32),
        compiler_params=pltpu.CompilerParams(
            dimension_semantics=("arbitrary",), vmem_limit_bytes=VMEM_LIMIT),
        name="mixer_out",
    )(x2, ya2, w_in, bg, lsg, lsb, wsp, bsp, wa, wb, wo, bo, g1, b1)


def _conv_ffn_kernel(h_ref, wup_ref, cw_ref, cb_ref, wd_ref, bd_ref, g2_ref, b2_ref, o_ref,
                     slab_ref, tail_ref, acc_ref, *, tiles_per_seq):
    tm = h_ref.shape[0]
    half = FF_SLABS // 2
    seq_start = (pl.program_id(0) % tiles_per_seq) == 0

    def produce(c, slot):
        for part in range(2):
            c0 = part * D_FF + c * FF_PAIR
            up = _dot(h_ref[...], wup_ref[:, c0:c0 + FF_PAIR])
            for s in range(half):
                slab_ref[slot, part * half + s, 8:tm + 8, :] = up[:, s * LANES:(s + 1) * LANES]

    def consume(c, slot):
        conv = []
        for s in range(FF_SLABS):
            c0 = (s // half) * D_FF + c * FF_PAIR + (s % half) * LANES
            cw = cw_ref[:, c0:c0 + LANES]
            slab_ref[slot, s, 0:8, :] = jnp.where(seq_start, 0.0, tail_ref[c, s])
            tail_ref[c, s] = slab_ref[slot, s, tm:tm + 8, :]
            conv.append(cb_ref[:, c0:c0 + LANES]
                        + cw[0:1] * slab_ref[slot, s, 6:tm + 6, :]
                        + cw[1:2] * slab_ref[slot, s, 7:tm + 7, :]
                        + cw[2:3] * slab_ref[slot, s, 8:tm + 8, :])
        act = jnp.concatenate([conv[s] * _sigmoid(conv[s]) * conv[half + s]
                               for s in range(half)], axis=1)
        acc_ref[...] += _dot(act, wd_ref[c * FF_PAIR:(c + 1) * FF_PAIR, :])

    acc_ref[...] = DN_ALPHA * h_ref[...] + bd_ref[...]
    produce(0, 0)
    for c in range(FF_STEPS):
        if c + 1 < FF_STEPS:
            produce(c + 1, (c + 1) % 2)
        consume(c, c % 2)
    o_ref[...] = _ln(acc_ref[...], g2_ref[...], b2_ref[...])


def _conv_ffn(h2, wup, cw, cb, wd, bd, g2, b2, seq, tm):
    tokens = h2.shape[0]
    return pl.pallas_call(
        functools.partial(_conv_ffn_kernel, tiles_per_seq=seq // tm),
        grid=(tokens // tm,),
        in_specs=[
            pl.BlockSpec((tm, D_MODEL), lambda i: (i, 0)),
            _resident(wup.shape), _resident(cw.shape), _resident(cb.shape), _resident(wd.shape),
            _resident(bd.shape), _resident(g2.shape), _resident(b2.shape),
        ],
        out_specs=pl.BlockSpec((tm, D_MODEL), lambda i: (i, 0)),
        out_shape=jax.ShapeDtypeStruct((tokens, D_MODEL), F32),
        scratch_shapes=[
            pltpu.VMEM((2, FF_SLABS, tm + 8, LANES), F32),
            pltpu.VMEM((FF_STEPS, FF_SLABS, 8, LANES), F32),
            pltpu.VMEM((tm, D_MODEL), F32),
        ],
        compiler_params=pltpu.CompilerParams(
            dimension_semantics=("arbitrary",), vmem_limit_bytes=VMEM_LIMIT),
        name="conv_ffn",
    )(h2, wup, cw, cb, wd, bd, g2, b2)


def _tri_const():
    j = jnp.arange(Q_BLOCK)[:, None]
    s = jnp.arange(Q_BLOCK)[None, :]
    half = jnp.concatenate([-(j >= s).astype(F32), -jnp.ones((Q_BLOCK, Q_BLOCK), F32)], axis=1)
    return jnp.concatenate([half, half], axis=0).astype(BF16)


def _layer(h, w_in, b_gate, ln_sg_g, ln_sg_b, w_spatial, b_spatial, w_branch_a, w_branch_b, w_out,
           b_out, ln1_g, ln1_b, w_up, conv_w, conv_b, w_down, b_down, ln2_g, ln2_b):
    batch, seq, _ = h.shape
    tokens = batch * seq
    x2 = h.reshape(tokens, D_MODEL)
    row = lambda v: v.reshape(1, -1)

    wkt = w_in[:, SB_WIDTH:2 * SB_WIDTH].T
    bsp = jnp.repeat(b_spatial.T, SG_WIDTH // N_SG_GROUPS, axis=1)

    q2, kt3, v2 = _qkv_proj(x2, w_in, wkt, batch, seq, tm=1024)
    ya3 = _sb_attention(q2.reshape(batch, seq, SB_WIDTH), kt3, v2.reshape(batch, seq, SB_WIDTH),
                        _tri_const(), sub=8, npair=2)
    h1 = _mixer_out(x2, ya3.reshape(tokens, SB_WIDTH), w_in, row(b_gate), row(ln_sg_g),
                    row(ln_sg_b), w_spatial, bsp, w_branch_a.astype(BF16), w_branch_b, w_out,
                    row(b_out), row(ln1_g), row(ln1_b), tm=512)
    h2 = _conv_ffn(h1, w_up, conv_w, row(conv_b), w_down, row(b_down), row(ln2_g), row(ln2_b), seq,
                   tm=FFN_TM)
    return h2.reshape(batch, seq, D_MODEL)


def kernel(x, w_in, b_gate, ln_sg_g, ln_sg_b, w_spatial, b_spatial, w_branch_a, w_branch_b, w_out,
           b_out, ln1_g, ln1_b, w_up, conv_w, conv_b, w_down, b_down, ln2_g, ln2_b):
    h = x
    for l in range(w_in.shape[0]):
        h = _layer(h, w_in[l], b_gate[l], ln_sg_g[l], ln_sg_b[l], w_spatial[l], b_spatial[l],
                   w_branch_a[l], w_branch_b[l], w_out[l], b_out[l], ln1_g[l], ln1_b[l], w_up[l],
                   conv_w[l], conv_b[l], w_down[l], b_down[l], ln2_g[l], ln2_b[l])
    return h
```

```python
import functools

import jax
import jax.numpy as jnp
from jax import lax
from jax.experimental import pallas as pl
from jax.experimental.pallas import tpu as pltpu

F32 = jnp.float32
BF16 = jnp.bfloat16

D_MODEL = 1024
N_HEADS = 8
HEAD_DIM = 64
SB_WIDTH = N_HEADS * HEAD_DIM
SG_WIDTH = 512
COL_U = 3 * SB_WIDTH
COL_VSG = COL_U + SG_WIDTH
COL_GATE = COL_VSG + SG_WIDTH
N_SG_GROUPS = 8
CHUNK = 128
Q_BLOCK = 128
D_FF = 2816
LN_EPS = 1e-5
LOG2E = 1.4426950408889634
STICK_CUTOFF = 64.0 * LOG2E
DN_ALPHA = 2.0 ** 0.25

LANES = 128
PAIR = 2 * HEAD_DIM
N_PAIRS = N_HEADS // 2
FF_PAIR = 256
FF_STEPS = D_FF // FF_PAIR
FF_SLABS = 2 * FF_PAIR // LANES

QKV_TM = 1024
ATTN_SUB = 16
ATTN_NPAIR = 2
MIXER_TM = 512
FFN_TM = 512
VMEM_LIMIT = 56 * 1024 * 1024


def _ln(x, g, b):
    mu = jnp.mean(x, axis=-1, keepdims=True)
    xc = x - mu
    var = jnp.mean(xc * xc, axis=-1, keepdims=True)
    return xc * lax.rsqrt(var + LN_EPS) * g + b


def _gelu_tanh(x):
    c = 0.7978845608028654
    return x * (0.5 * (1.0 + jnp.tanh(c * (x + 0.044715 * (x * x * x)))))


def _sigmoid(x):
    return 1.0 / (1.0 + jnp.exp(-x))


def _dot(a, b):
    return jnp.dot(a, b, preferred_element_type=F32)


def _resident(shape):
    nd = len(shape)
    return pl.BlockSpec(shape, lambda *_: (0,) * nd, pipeline_mode=pl.Buffered(1))


def _qkv_kernel(x_ref, wq_ref, wkt_ref, wv_ref, q_ref, kt_ref, v_ref):
    x = x_ref[...]
    q_ref[...] = (_dot(x, wq_ref[...]) * (0.125 * LOG2E)).astype(BF16)
    v_ref[...] = _dot(x, wv_ref[...]).astype(BF16)
    kt = lax.dot_general(wkt_ref[...], x, (((1,), (1,)), ((), ())),
                         preferred_element_type=F32)
    kt_ref[0] = kt.astype(BF16)


def _qkv_proj(x2, w_in, wkt, batch, seq, tm):
    tokens = x2.shape[0]
    tiles_per_seq = seq // tm
    col_block = lambda k: pl.BlockSpec((D_MODEL, SB_WIDTH), lambda i: (0, k), pipeline_mode=pl.Buffered(1))
    return pl.pallas_call(
        _qkv_kernel,
        grid=(tokens // tm,),
        in_specs=[
            pl.BlockSpec((tm, D_MODEL), lambda i: (i, 0)),
            col_block(0),
            _resident((SB_WIDTH, D_MODEL)),
            col_block(2),
        ],
        out_specs=[
            pl.BlockSpec((tm, SB_WIDTH), lambda i: (i, 0)),
            pl.BlockSpec((1, SB_WIDTH, tm), lambda i: (i // tiles_per_seq, 0, i % tiles_per_seq)),
            pl.BlockSpec((tm, SB_WIDTH), lambda i: (i, 0)),
        ],
        out_shape=[
            jax.ShapeDtypeStruct((tokens, SB_WIDTH), BF16),
            jax.ShapeDtypeStruct((batch, SB_WIDTH, seq), BF16),
            jax.ShapeDtypeStruct((tokens, SB_WIDTH), BF16),
        ],
        compiler_params=pltpu.CompilerParams(
            dimension_semantics=("arbitrary",), vmem_limit_bytes=VMEM_LIMIT),
        name="qkv_proj",
    )(x2, w_in, wkt, w_in)


def _softplus2(z):
    return jnp.maximum(z, 0.0) + jnp.log(1.0 + jnp.exp2(-jnp.abs(z))) * LOG2E


def _sb_attn_kernel(q_ref, kt_ref, v_ref, tri_ref, o_ref, kbd_ref, vbd_ref, pre_ref, rs_ref,
                    carry_ref, acc_ref, *, n_kblocks, sub, npair):
    qs = pl.program_id(2)
    mq = sub * Q_BLOCK
    pairs = range(npair)

    @pl.when(qs == 0)
    def _():
        row_lo = lax.broadcasted_iota(jnp.int32, (PAIR, Q_BLOCK), 0) < HEAD_DIM
        lane_lo = lax.broadcasted_iota(jnp.int32, (Q_BLOCK, PAIR), 1) < HEAD_DIM
        zero = jnp.zeros((PAIR, Q_BLOCK), BF16)
        for p in pairs:
            for j in range(n_kblocks):
                kt = kt_ref[0, p * PAIR:(p + 1) * PAIR, j * Q_BLOCK:(j + 1) * Q_BLOCK]
                kbd_ref[p, j, :, 0:Q_BLOCK] = jnp.where(row_lo, kt, zero)
                kbd_ref[p, j, :, Q_BLOCK:2 * Q_BLOCK] = jnp.where(row_lo, zero, kt)
                vt = v_ref[0, j * Q_BLOCK:(j + 1) * Q_BLOCK, p * PAIR:(p + 1) * PAIR]
                vbd_ref[p, j, 0:Q_BLOCK, :] = jnp.where(lane_lo, vt, zero)
                vbd_ref[p, j, Q_BLOCK:2 * Q_BLOCK, :] = jnp.where(lane_lo, zero, vt)
            vbd_ref[p, n_kblocks] = jnp.zeros((2 * Q_BLOCK, PAIR), BF16)

    tri = tri_ref[...]
    row = lax.broadcasted_iota(jnp.int32, (Q_BLOCK, 2 * Q_BLOCK), 0)
    col = lax.broadcasted_iota(jnp.int32, (Q_BLOCK, 2 * Q_BLOCK), 1) & (Q_BLOCK - 1)
    causal = col < row

    def mask_top(x, fill, diag):
        if not diag:
            return x
        top = jnp.where(causal, x[:Q_BLOCK], fill)
        return top if x.shape[0] == Q_BLOCK else jnp.concatenate([top, x[Q_BLOCK:]], axis=0)

    def stage1(p, r0, r1, j, diag):
        q2 = q_ref[0, r0:r1, p * PAIR:(p + 1) * PAIR]
        z = _dot(q2, kbd_ref[p, j])
        sp = mask_top(_softplus2(z), 0.0, diag)
        hi = sp.astype(BF16)
        lo = (sp - hi.astype(F32)).astype(BF16)
        cr = [_dot(jnp.concatenate([hi[:, h * Q_BLOCK:(h + 1) * Q_BLOCK],
                                    lo[:, h * Q_BLOCK:(h + 1) * Q_BLOCK]], axis=1), tri)
              for h in range(2)]
        pre = z + jnp.concatenate([cr[0][:, :Q_BLOCK], cr[1][:, :Q_BLOCK]], axis=1)
        rs = jnp.concatenate([cr[0][:, Q_BLOCK:], cr[1][:, Q_BLOCK:]], axis=1)
        return pre, rs

    def stage2(p, pre, rs, r0, r1, j, diag, jv=None):
        c = carry_ref[p, r0:r1, :]
        c_new = c + rs
        carry_ref[p, r0:r1, :] = c_new
        w = jnp.exp2(mask_top(pre + c, -1e30, diag)).astype(BF16)
        acc_ref[p, r0:r1, :] += _dot(w, vbd_ref[p, j if jv is None else jv])
        return jnp.max(c_new[-Q_BLOCK:])

    def block(p, r0, r1, j, diag, jv=None):
        pre, rs = stage1(p, r0, r1, j, diag)
        return stage2(p, pre, rs, r0, r1, j, diag, jv)

    def stick_left():
        return functools.reduce(jnp.maximum, [jnp.max(carry_ref[p]) for p in pairs])

    carry_ref[...] = jnp.zeros_like(carry_ref)
    acc_ref[...] = jnp.zeros_like(acc_ref)
    j0 = qs * sub
    j1 = jnp.maximum(j0 - 1, 0)
    jv = jnp.where(qs > 0, j1, n_kblocks)

    stick = []
    for d in reversed(range(-1, sub)):
        r0, r1 = max(d, 0) * Q_BLOCK, min(d + 2, sub) * Q_BLOCK
        for p in pairs:
            left = block(p, r0, r1, j0 + d, True) if d >= 0 else block(p, r0, r1, j1, False, jv)
            if d < sub - 1:
                stick.append(left)

    @pl.when(functools.reduce(jnp.maximum, stick) >= -STICK_CUTOFF)
    def _():
        for d in reversed(range(0, sub - 2)):
            for p in pairs:
                block(p, (d + 2) * Q_BLOCK, mq, j0 + d, False)

        @pl.when(qs > 0)
        def _():
            for p in pairs:
                block(p, Q_BLOCK, mq, j0 - 1, False)

            @pl.when(jnp.logical_and(j0 >= 2, stick_left() >= -STICK_CUTOFF))
            def _():
                for p in pairs:
                    pre, rs = stage1(p, 0, mq, j0 - 2, False)
                    pre_ref[p] = pre
                    rs_ref[p] = rs

                def cond(state):
                    j, alive = state
                    return jnp.logical_and(j >= 1, alive > 0)

                def body(state):
                    j, _ = state
                    for p in pairs:
                        stage2(p, pre_ref[p], rs_ref[p], 0, mq, j, False)
                        pre_n, rs_n = stage1(p, 0, mq, j - 1, False)
                        pre_ref[p] = pre_n
                        rs_ref[p] = rs_n
                    return j - 1, (stick_left() >= -STICK_CUTOFF).astype(jnp.int32)

                j_end, alive = lax.while_loop(cond, body, (j0 - 2, jnp.int32(1)))

                @pl.when(jnp.logical_and(j_end == 0, alive > 0))
                def _():
                    for p in pairs:
                        stage2(p, pre_ref[p], rs_ref[p], 0, mq, 0, False)

    for p in pairs:
        o_ref[0, :, p * PAIR:(p + 1) * PAIR] = acc_ref[p].astype(BF16)


def _sb_attention(q3, kt3, v3, tri, sub, npair):
    batch, seq, _ = q3.shape
    n_blocks = seq // Q_BLOCK
    mq = sub * Q_BLOCK
    width = npair * PAIR
    return pl.pallas_call(
        functools.partial(_sb_attn_kernel, n_kblocks=n_blocks, sub=sub, npair=npair),
        grid=(batch, N_PAIRS // npair, seq // mq),
        in_specs=[
            pl.BlockSpec((1, mq, width), lambda b, g, i: (b, i, g)),
            pl.BlockSpec((1, width, seq), lambda b, g, i: (b, g, 0)),
            pl.BlockSpec((1, seq, width), lambda b, g, i: (b, 0, g)),
            _resident((2 * Q_BLOCK, 2 * Q_BLOCK)),
        ],
        out_specs=pl.BlockSpec((1, mq, width), lambda b, g, i: (b, i, g)),
        out_shape=jax.ShapeDtypeStruct((batch, seq, SB_WIDTH), BF16),
        scratch_shapes=[
            pltpu.VMEM((npair, n_blocks, PAIR, 2 * Q_BLOCK), BF16),
            pltpu.VMEM((npair, n_blocks + 1, 2 * Q_BLOCK, PAIR), BF16),
            pltpu.VMEM((npair, mq, 2 * Q_BLOCK), F32),
            pltpu.VMEM((npair, mq, 2 * Q_BLOCK), F32),
            pltpu.VMEM((npair, mq, 2 * Q_BLOCK), F32),
            pltpu.VMEM((npair, mq, PAIR), F32),
        ],
        compiler_params=pltpu.CompilerParams(
            dimension_semantics=("arbitrary", "arbitrary", "arbitrary"),
            vmem_limit_bytes=VMEM_LIMIT),
        name="sb_attn",
    )(q3, kt3, v3, tri)


def _mixer_out_kernel(x_ref, ya_ref, win_ref, bg_ref, lsg_ref, lsb_ref, wsp_ref,
                      bsp_ref, wa_ref, wb_ref, wo_ref, bo_ref, g1_ref, b1_ref, o_ref):
    x = x_ref[...]
    tm = x.shape[0]

    u = _gelu_tanh(_dot(x, win_ref[:, COL_U:COL_VSG]))
    vs = _ln(_gelu_tanh(_dot(x, win_ref[:, COL_VSG:COL_GATE])), lsg_ref[...], lsb_ref[...]).astype(BF16)

    tril = (lax.broadcasted_iota(jnp.int32, (CHUNK, CHUNK), 1)
            <= lax.broadcasted_iota(jnp.int32, (CHUNK, CHUNK), 0))
    wsp = [jnp.where(tril, wsp_ref[g], 0.0).astype(BF16) for g in range(N_SG_GROUPS)]
    lane_lo = lax.broadcasted_iota(jnp.int32, (CHUNK, LANES), 1) < HEAD_DIM
    zero = jnp.zeros((CHUNK, LANES), BF16)
    rows = []
    for c in range(tm // CHUNK):
        parts = []
        for p in range(N_SG_GROUPS // 2):
            v2 = vs[c * CHUNK:(c + 1) * CHUNK, p * LANES:(p + 1) * LANES]
            parts.append(_dot(wsp[2 * p], jnp.where(lane_lo, v2, zero))
                         + _dot(wsp[2 * p + 1], jnp.where(lane_lo, zero, v2)))
        rows.append(jnp.concatenate(parts, axis=1) + bsp_ref[...])
    yb = u * jnp.concatenate(rows, axis=0)

    g = _sigmoid(_dot(x, win_ref[:, COL_GATE:]) + bg_ref[...])
    merged = (g[:, :D_MODEL] * _dot(ya_ref[...], wa_ref[...])
              + g[:, D_MODEL:] * _dot(yb, wb_ref[...]))
    y = _dot(merged, wo_ref[...]) + bo_ref[...]
    o_ref[...] = _ln(DN_ALPHA * x + y, g1_ref[...], b1_ref[...])


def _mixer_out(x2, ya2, w_in, bg, lsg, lsb, wsp, bsp, wa, wb, wo, bo, g1, b1, tm):
    tokens = x2.shape[0]
    return pl.pallas_call(
        _mixer_out_kernel,
        grid=(tokens // tm,),
        in_specs=[
            pl.BlockSpec((tm, D_MODEL), lambda i: (i, 0)),
            pl.BlockSpec((tm, SB_WIDTH), lambda i: (i, 0)),
            _resident(w_in.shape), _resident(bg.shape),
            _resident(lsg.shape), _resident(lsb.shape), _resident(wsp.shape), _resident(bsp.shape),
            _resident(wa.shape), _resident(wb.shape), _resident(wo.shape), _resident(bo.shape),
            _resident(g1.shape), _resident(b1.shape),
        ],
        out_specs=pl.BlockSpec((tm, D_MODEL), lambda i: (i, 0)),
        out_shape=jax.ShapeDtypeStruct((tokens, D_MODEL), F32),
        compiler_params=pltpu.CompilerParams(
            dimension_semantics=("arbitrary",), vmem_limit_bytes=VMEM_LIMIT),
        name="mixer_out",
    )(x2, ya2, w_in, bg, lsg, lsb, wsp, bsp, wa, wb, wo, bo, g1, b1)


def _conv_ffn_kernel(h_ref, wup_ref, cw_ref, cb_ref, wd_ref, bd_ref, g2_ref, b2_ref, o_ref,
                     slab_ref, tail_ref, acc_ref, *, tiles_per_seq):
    tm = h_ref.shape[0]
    half = FF_SLABS // 2
    seq_start = (pl.program_id(0) % tiles_per_seq) == 0

    def produce(c, slot):
        for part in range(2):
            c0 = part * D_FF + c * FF_PAIR
            up = _dot(h_ref[...], wup_ref[:, c0:c0 + FF_PAIR])
            for s in range(half):
                slab_ref[slot, part * half + s, 8:tm + 8, :] = up[:, s * LANES:(s + 1) * LANES]

    def consume(c, slot):
        conv = []
        for s in range(FF_SLABS):
            c0 = (s // half) * D_FF + c * FF_PAIR + (s % half) * LANES
            cw = cw_ref[:, c0:c0 + LANES]
            slab_ref[slot, s, 0:8, :] = jnp.where(seq_start, 0.0, tail_ref[c, s])
            tail_ref[c, s] = slab_ref[slot, s, tm:tm + 8, :]
            conv.append(cb_ref[:, c0:c0 + LANES]
                        + cw[0:1] * slab_ref[slot, s, 6:tm + 6, :]
                        + cw[1:2] * slab_ref[slot, s, 7:tm + 7, :]
                        + cw[2:3] * slab_ref[slot, s, 8:tm + 8, :])
        act = jnp.concatenate([conv[s] * _sigmoid(conv[s]) * conv[half + s]
                               for s in range(half)], axis=1)
        acc_ref[...] += _dot(act, wd_ref[c * FF_PAIR:(c + 1) * FF_PAIR, :])

    acc_ref[...] = DN_ALPHA * h_ref[...] + bd_ref[...]
    produce(0, 0)
    for c in range(FF_STEPS):
        if c + 1 < FF_STEPS:
            produce(c + 1, (c + 1) % 2)
        consume(c, c % 2)
    o_ref[...] = _ln(acc_ref[...], g2_ref[...], b2_ref[...])


def _conv_ffn(h2, wup, cw, cb, wd, bd, g2, b2, seq, tm):
    tokens = h2.shape[0]
    return pl.pallas_call(
        functools.partial(_conv_ffn_kernel, tiles_per_seq=seq // tm),
        grid=(tokens // tm,),
        in_specs=[
            pl.BlockSpec((tm, D_MODEL), lambda i: (i, 0)),
            _resident(wup.shape), _resident(cw.shape), _resident(cb.shape), _resident(wd.shape),
            _resident(bd.shape), _resident(g2.shape), _resident(b2.shape),
        ],
        out_specs=pl.BlockSpec((tm, D_MODEL), lambda i: (i, 0)),
        out_shape=jax.ShapeDtypeStruct((tokens, D_MODEL), F32),
        scratch_shapes=[
            pltpu.VMEM((2, FF_SLABS, tm + 8, LANES), F32),
            pltpu.VMEM((FF_STEPS, FF_SLABS, 8, LANES), F32),
            pltpu.VMEM((tm, D_MODEL), F32),
        ],
        compiler_params=pltpu.CompilerParams(
            dimension_semantics=("arbitrary",), vmem_limit_bytes=VMEM_LIMIT),
        name="conv_ffn",
    )(h2, wup, cw, cb, wd, bd, g2, b2)


def _tri_const():
    j = jnp.arange(Q_BLOCK)[:, None]
    s = jnp.arange(Q_BLOCK)[None, :]
    half = jnp.concatenate([-(j >= s).astype(F32), -jnp.ones((Q_BLOCK, Q_BLOCK), F32)], axis=1)
    return jnp.concatenate([half, half], axis=0).astype(BF16)


def _layer(h, w_in, b_gate, ln_sg_g, ln_sg_b, w_spatial, b_spatial, w_branch_a, w_branch_b, w_out,
           b_out, ln1_g, ln1_b, w_up, conv_w, conv_b, w_down, b_down, ln2_g, ln2_b):
    batch, seq, _ = h.shape
    tokens = batch * seq
    x2 = h.reshape(tokens, D_MODEL)
    row = lambda v: v.reshape(1, -1)

    wkt = w_in[:, SB_WIDTH:2 * SB_WIDTH].T
    bsp = jnp.repeat(b_spatial.T, SG_WIDTH // N_SG_GROUPS, axis=1)

    q2, kt3, v2 = _qkv_proj(x2, w_in, wkt, batch, seq, tm=QKV_TM)
    ya3 = _sb_attention(q2.reshape(batch, seq, SB_WIDTH), kt3, v2.reshape(batch, seq, SB_WIDTH),
                        _tri_const(), sub=ATTN_SUB, npair=ATTN_NPAIR)
    h1 = _mixer_out(x2, ya3.reshape(tokens, SB_WIDTH), w_in, row(b_gate), row(ln_sg_g),
                    row(ln_sg_b), w_spatial, bsp, w_branch_a.astype(BF16), w_branch_b, w_out,
                    row(b_out), row(ln1_g), row(ln1_b), tm=MIXER_TM)
    h2 = _conv_ffn(h1, w_up, conv_w, row(conv_b), w_down, row(b_down), row(ln2_g), row(ln2_b), seq,
                   tm=FFN_TM)
    return h2.reshape(batch, seq, D_MODEL)


def kernel(x, w_in, b_gate, ln_sg_g, ln_sg_b, w_spatial, b_spatial, w_branch_a, w_branch_b, w_out,
           b_out, ln1_g, ln1_b, w_up, conv_w, conv_b, w_down, b_down, ln2_g, ln2_b):
    h = x
    for l in range(w_in.shape[0]):
        h = _layer(h, w_in[l], b_gate[l], ln_sg_g[l], ln_sg_b[l], w_spatial[l], b_spatial[l],
                   w_branch_a[l], w_branch_b[l], w_out[l], b_out[l], ln1_g[l], ln1_b[l], w_up[l],
                   conv_w[l], conv_b[l], w_down[l], b_down[l], ln2_g[l], ln2_b[l])
    return h
```

```python
import functools

import jax
import jax.numpy as jnp
from jax import lax
from jax.experimental import pallas as pl
from jax.experimental.pallas import tpu as pltpu

F32 = jnp.float32
BF16 = jnp.bfloat16

D_MODEL = 1024
N_HEADS = 8
HEAD_DIM = 64
SB_WIDTH = N_HEADS * HEAD_DIM
SG_WIDTH = 512
COL_U = 3 * SB_WIDTH
COL_VSG = COL_U + SG_WIDTH
COL_GATE = COL_VSG + SG_WIDTH
N_SG_GROUPS = 8
CHUNK = 128
Q_BLOCK = 128
D_FF = 2816
LN_EPS = 1e-5
LOG2E = 1.4426950408889634
STICK_CUTOFF = 64.0 * LOG2E
DN_ALPHA = 2.0 ** 0.25

LANES = 128
PAIR = 2 * HEAD_DIM
N_PAIRS = N_HEADS // 2
FF_PAIR = 256
FF_STEPS = D_FF // FF_PAIR
FF_SLABS = 2 * FF_PAIR // LANES

QKV_TM = 1024
ATTN_SUB = 8
ATTN_NPAIR = 2
MIXER_TM = 512
FFN_TM = 512
VMEM_LIMIT = 56 * 1024 * 1024


def _ln(x, g, b):
    mu = jnp.mean(x, axis=-1, keepdims=True)
    xc = x - mu
    var = jnp.mean(xc * xc, axis=-1, keepdims=True)
    return xc * lax.rsqrt(var + LN_EPS) * g + b


def _gelu_tanh(x):
    c = 0.7978845608028654
    return x * (0.5 * (1.0 + jnp.tanh(c * (x + 0.044715 * (x * x * x)))))


def _sigmoid(x):
    return 1.0 / (1.0 + jnp.exp(-x))


def _dot(a, b):
    return jnp.dot(a, b, preferred_element_type=F32)


def _resident(shape):
    nd = len(shape)
    return pl.BlockSpec(shape, lambda *_: (0,) * nd, pipeline_mode=pl.Buffered(1))


def _qkv_kernel(x_ref, wq_ref, wkt_ref, wv_ref, q_ref, kt_ref, v_ref):
    x = x_ref[...]
    q_ref[...] = (_dot(x, wq_ref[...]) * (0.125 * LOG2E)).astype(BF16)
    v_ref[...] = _dot(x, wv_ref[...]).astype(BF16)
    kt = lax.dot_general(wkt_ref[...], x, (((1,), (1,)), ((), ())),
                         preferred_element_type=F32)
    kt_ref[0] = kt.astype(BF16)


def _qkv_proj(x2, w_in, wkt, batch, seq, tm):
    tokens = x2.shape[0]
    tiles_per_seq = seq // tm
    col_block = lambda k: pl.BlockSpec((D_MODEL, SB_WIDTH), lambda i: (0, k), pipeline_mode=pl.Buffered(1))
    return pl.pallas_call(
        _qkv_kernel,
        grid=(tokens // tm,),
        in_specs=[
            pl.BlockSpec((tm, D_MODEL), lambda i: (i, 0)),
            col_block(0),
            _resident((SB_WIDTH, D_MODEL)),
            col_block(2),
        ],
        out_specs=[
            pl.BlockSpec((tm, SB_WIDTH), lambda i: (i, 0)),
            pl.BlockSpec((1, SB_WIDTH, tm), lambda i: (i // tiles_per_seq, 0, i % tiles_per_seq)),
            pl.BlockSpec((tm, SB_WIDTH), lambda i: (i, 0)),
        ],
        out_shape=[
            jax.ShapeDtypeStruct((tokens, SB_WIDTH), BF16),
            jax.ShapeDtypeStruct((batch, SB_WIDTH, seq), BF16),
            jax.ShapeDtypeStruct((tokens, SB_WIDTH), BF16),
        ],
        compiler_params=pltpu.CompilerParams(
            dimension_semantics=("arbitrary",), vmem_limit_bytes=VMEM_LIMIT),
        name="qkv_proj",
    )(x2, w_in, wkt, w_in)


def _softplus2(z):
    return jnp.maximum(z, 0.0) + jnp.log(1.0 + jnp.exp2(-jnp.abs(z))) * LOG2E


def _sb_attn_kernel(q_ref, kt_ref, v_ref, tri_ref, o_ref, kbd_ref, vbd_ref, pre_ref, rs_ref,
                    carry_ref, acc_ref, *, n_kblocks, sub, npair):
    qs = pl.program_id(2)
    mq = sub * Q_BLOCK
    pairs = range(npair)

    @pl.when(qs == 0)
    def _():
        row_lo = lax.broadcasted_iota(jnp.int32, (PAIR, Q_BLOCK), 0) < HEAD_DIM
        lane_lo = lax.broadcasted_iota(jnp.int32, (Q_BLOCK, PAIR), 1) < HEAD_DIM
        zero = jnp.zeros((PAIR, Q_BLOCK), BF16)
        for p in pairs:
            for j in range(n_kblocks):
                kt = kt_ref[0, p * PAIR:(p + 1) * PAIR, j * Q_BLOCK:(j + 1) * Q_BLOCK]
                kbd_ref[p, j, :, 0:Q_BLOCK] = jnp.where(row_lo, kt, zero)
                kbd_ref[p, j, :, Q_BLOCK:2 * Q_BLOCK] = jnp.where(row_lo, zero, kt)
                vt = v_ref[0, j * Q_BLOCK:(j + 1) * Q_BLOCK, p * PAIR:(p + 1) * PAIR]
                vbd_ref[p, j, 0:Q_BLOCK, :] = jnp.where(lane_lo, vt, zero)
                vbd_ref[p, j, Q_BLOCK:2 * Q_BLOCK, :] = jnp.where(lane_lo, zero, vt)
            vbd_ref[p, n_kblocks] = jnp.zeros((2 * Q_BLOCK, PAIR), BF16)

    tri = tri_ref[...]
    row = lax.broadcasted_iota(jnp.int32, (Q_BLOCK, 2 * Q_BLOCK), 0)
    col = lax.broadcasted_iota(jnp.int32, (Q_BLOCK, 2 * Q_BLOCK), 1) & (Q_BLOCK - 1)
    causal = col < row

    def mask_top(x, fill, diag):
        if not diag:
            return x
        top = jnp.where(causal, x[:Q_BLOCK], fill)
        return top if x.shape[0] == Q_BLOCK else jnp.concatenate([top, x[Q_BLOCK:]], axis=0)

    def stage1(p, r0, r1, j, diag):
        q2 = q_ref[0, r0:r1, p * PAIR:(p + 1) * PAIR]
        z = _dot(q2, kbd_ref[p, j])
        sp = mask_top(_softplus2(z), 0.0, diag)
        hi = sp.astype(BF16)
        lo = (sp - hi.astype(F32)).astype(BF16)
        cr = [_dot(jnp.concatenate([hi[:, h * Q_BLOCK:(h + 1) * Q_BLOCK],
                                    lo[:, h * Q_BLOCK:(h + 1) * Q_BLOCK]], axis=1), tri)
              for h in range(2)]
        pre = z + jnp.concatenate([cr[0][:, :Q_BLOCK], cr[1][:, :Q_BLOCK]], axis=1)
        rs = jnp.concatenate([cr[0][:, Q_BLOCK:], cr[1][:, Q_BLOCK:]], axis=1)
        return pre, rs

    def stage2(p, pre, rs, r0, r1, j, diag, jv=None):
        c = carry_ref[p, r0:r1, :]
        c_new = c + rs
        carry_ref[p, r0:r1, :] = c_new
        w = jnp.exp2(mask_top(pre + c, -1e30, diag)).astype(BF16)
        acc_ref[p, r0:r1, :] += _dot(w, vbd_ref[p, j if jv is None else jv])
        return jnp.max(c_new[-Q_BLOCK:])

    def block(p, r0, r1, j, diag, jv=None):
        pre, rs = stage1(p, r0, r1, j, diag)
        return stage2(p, pre, rs, r0, r1, j, diag, jv)

    def stick_left():
        return functools.reduce(jnp.maximum, [jnp.max(carry_ref[p]) for p in pairs])

    carry_ref[...] = jnp.zeros_like(carry_ref)
    acc_ref[...] = jnp.zeros_like(acc_ref)
    j0 = qs * sub
    j1 = jnp.maximum(j0 - 1, 0)
    jv = jnp.where(qs > 0, j1, n_kblocks)

    stick = []
    for d in reversed(range(-1, sub)):
        r0, r1 = max(d, 0) * Q_BLOCK, min(d + 2, sub) * Q_BLOCK
        for p in pairs:
            left = block(p, r0, r1, j0 + d, True) if d >= 0 else block(p, r0, r1, j1, False, jv)
            if d < sub - 1:
                stick.append(left)

    @pl.when(functools.reduce(jnp.maximum, stick) >= -STICK_CUTOFF)
    def _():
        for d in reversed(range(0, sub - 2)):
            for p in pairs:
                block(p, (d + 2) * Q_BLOCK, mq, j0 + d, False)

        @pl.when(qs > 0)
        def _():
            for p in pairs:
                block(p, Q_BLOCK, mq, j0 - 1, False)

            @pl.when(jnp.logical_and(j0 >= 2, stick_left() >= -STICK_CUTOFF))
            def _():
                for p in pairs:
                    pre, rs = stage1(p, 0, mq, j0 - 2, False)
                    pre_ref[p] = pre
                    rs_ref[p] = rs

                def cond(state):
                    j, alive = state
                    return jnp.logical_and(j >= 1, alive > 0)

                def body(state):
                    j, _ = state
                    for p in pairs:
                        stage2(p, pre_ref[p], rs_ref[p], 0, mq, j, False)
                        pre_n, rs_n = stage1(p, 0, mq, j - 1, False)
                        pre_ref[p] = pre_n
                        rs_ref[p] = rs_n
                    return j - 1, (stick_left() >= -STICK_CUTOFF).astype(jnp.int32)

                j_end, alive = lax.while_loop(cond, body, (j0 - 2, jnp.int32(1)))

                @pl.when(jnp.logical_and(j_end == 0, alive > 0))
                def _():
                    for p in pairs:
                        stage2(p, pre_ref[p], rs_ref[p], 0, mq, 0, False)

    for p in pairs:
        o_ref[0, :, p * PAIR:(p + 1) * PAIR] = acc_ref[p].astype(BF16)


def _sb_attention(q3, kt3, v3, tri, sub, npair):
    batch, seq, _ = q3.shape
    n_blocks = seq // Q_BLOCK
    mq = sub * Q_BLOCK
    width = npair * PAIR
    return pl.pallas_call(
        functools.partial(_sb_attn_kernel, n_kblocks=n_blocks, sub=sub, npair=npair),
        grid=(batch, N_PAIRS // npair, seq // mq),
        in_specs=[
            pl.BlockSpec((1, mq, width), lambda b, g, i: (b, i, g)),
            pl.BlockSpec((1, width, seq), lambda b, g, i: (b, g, 0)),
            pl.BlockSpec((1, seq, width), lambda b, g, i: (b, 0, g)),
            _resident((2 * Q_BLOCK, 2 * Q_BLOCK)),
        ],
        out_specs=pl.BlockSpec((1, mq, width), lambda b, g, i: (b, i, g)),
        out_shape=jax.ShapeDtypeStruct((batch, seq, SB_WIDTH), BF16),
        scratch_shapes=[
            pltpu.VMEM((npair, n_blocks, PAIR, 2 * Q_BLOCK), BF16),
            pltpu.VMEM((npair, n_blocks + 1, 2 * Q_BLOCK, PAIR), BF16),
            pltpu.VMEM((npair, mq, 2 * Q_BLOCK), F32),
            pltpu.VMEM((npair, mq, 2 * Q_BLOCK), F32),
            pltpu.VMEM((npair, mq, 2 * Q_BLOCK), F32),
            pltpu.VMEM((npair, mq, PAIR), F32),
        ],
        compiler_params=pltpu.CompilerParams(
            dimension_semantics=("arbitrary", "arbitrary", "arbitrary"),
            vmem_limit_bytes=VMEM_LIMIT),
        name="sb_attn",
    )(q3, kt3, v3, tri)


def _mixer_out_kernel(x_ref, ya_ref, win_ref, bg_ref, lsg_ref, lsb_ref, wsp_ref,
                      bsp_ref, wa_ref, wb_ref, wo_ref, bo_ref, g1_ref, b1_ref, o_ref):
    x = x_ref[...]
    tm = x.shape[0]

    u = _gelu_tanh(_dot(x, win_ref[:, COL_U:COL_VSG]))
    vs = _ln(_gelu_tanh(_dot(x, win_ref[:, COL_VSG:COL_GATE])), lsg_ref[...], lsb_ref[...]).astype(BF16)

    tril = (lax.broadcasted_iota(jnp.int32, (CHUNK, CHUNK), 1)
            <= lax.broadcasted_iota(jnp.int32, (CHUNK, CHUNK), 0))
    wsp = [jnp.where(tril, wsp_ref[g], 0.0).astype(BF16) for g in range(N_SG_GROUPS)]
    lane_lo = lax.broadcasted_iota(jnp.int32, (CHUNK, LANES), 1) < HEAD_DIM
    zero = jnp.zeros((CHUNK, LANES), BF16)
    rows = []
    for c in range(tm // CHUNK):
        parts = []
        for p in range(N_SG_GROUPS // 2):
            v2 = vs[c * CHUNK:(c + 1) * CHUNK, p * LANES:(p + 1) * LANES]
            parts.append(_dot(wsp[2 * p], jnp.where(lane_lo, v2, zero))
                         + _dot(wsp[2 * p + 1], jnp.where(lane_lo, zero, v2)))
        rows.append(jnp.concatenate(parts, axis=1) + bsp_ref[...])
    yb = u * jnp.concatenate(rows, axis=0)

    g = _sigmoid(_dot(x, win_ref[:, COL_GATE:]) + bg_ref[...])
    merged = (g[:, :D_MODEL] * _dot(ya_ref[...], wa_ref[...])
              + g[:, D_MODEL:] * _dot(yb, wb_ref[...]))
    y = _dot(merged, wo_ref[...]) + bo_ref[...]
    o_ref[...] = _ln(DN_ALPHA * x + y, g1_ref[...], b1_ref[...])


def _mixer_out(x2, ya2, w_in, bg, lsg, lsb, wsp, bsp, wa, wb, wo, bo, g1, b1, tm):
    tokens = x2.shape[0]
    return pl.pallas_call(
        _mixer_out_kernel,
        grid=(tokens // tm,),
        in_specs=[
            pl.BlockSpec((tm, D_MODEL), lambda i: (i, 0)),
            pl.BlockSpec((tm, SB_WIDTH), lambda i: (i, 0)),
            _resident(w_in.shape), _resident(bg.shape),
            _resident(lsg.shape), _resident(lsb.shape), _resident(wsp.shape), _resident(bsp.shape),
            _resident(wa.shape), _resident(wb.shape), _resident(wo.shape), _resident(bo.shape),
            _resident(g1.shape), _resident(b1.shape),
        ],
        out_specs=pl.BlockSpec((tm, D_MODEL), lambda i: (i, 0)),
        out_shape=jax.ShapeDtypeStruct((tokens, D_MODEL), F32),
        compiler_params=pltpu.CompilerParams(
            dimension_semantics=("arbitrary",), vmem_limit_bytes=VMEM_LIMIT),
        name="mixer_out",
    )(x2, ya2, w_in, bg, lsg, lsb, wsp, bsp, wa, wb, wo, bo, g1, b1)


def _conv_ffn_kernel(h_ref, wup_ref, cw_ref, cb_ref, wd_ref, bd_ref, g2_ref, b2_ref, o_ref,
                     slab_ref, tail_ref, acc_ref, *, tiles_per_seq):
    tm = h_ref.shape[0]
    half = FF_SLABS // 2
    seq_start = (pl.program_id(0) % tiles_per_seq) == 0

    def produce(c, slot):
        for part in range(2):
            c0 = part * D_FF + c * FF_PAIR
            up = _dot(h_ref[...], wup_ref[:, c0:c0 + FF_PAIR])
            for s in range(half):
                slab_ref[slot, part * half + s, 8:tm + 8, :] = up[:, s * LANES:(s + 1) * LANES]

    def consume(c, slot):
        conv = []
        for s in range(FF_SLABS):
            c0 = (s // half) * D_FF + c * FF_PAIR + (s % half) * LANES
            cw = cw_ref[:, c0:c0 + LANES]
            slab_ref[slot, s, 0:8, :] = jnp.where(seq_start, 0.0, tail_ref[c, s])
            tail_ref[c, s] = slab_ref[slot, s, tm:tm + 8, :]
            conv.append(cb_ref[:, c0:c0 + LANES]
                        + cw[0:1] * slab_ref[slot, s, 6:tm + 6, :]
                        + cw[1:2] * slab_ref[slot, s, 7:tm + 7, :]
                        + cw[2:3] * slab_ref[slot, s, 8:tm + 8, :])
        act = jnp.concatenate([conv[s] * _sigmoid(conv[s]) * conv[half + s]
                               for s in range(half)], axis=1)
        acc_ref[...] += _dot(act, wd_ref[c * FF_PAIR:(c + 1) * FF_PAIR, :])

    acc_ref[...] = DN_ALPHA * h_ref[...] + bd_ref[...]
    produce(0, 0)
    for c in range(FF_STEPS):
        if c + 1 < FF_STEPS:
            produce(c + 1, (c + 1) % 2)
        consume(c, c % 2)
    o_ref[...] = _ln(acc_ref[...], g2_ref[...], b2_ref[...])


def _conv_ffn(h2, wup, cw, cb, wd, bd, g2, b2, seq, tm):
    tokens = h2.shape[0]
    return pl.pallas_call(
        functools.partial(_conv_ffn_kernel, tiles_per_seq=seq // tm),
        grid=(tokens // tm,),
        in_specs=[
            pl.BlockSpec((tm, D_MODEL), lambda i: (i, 0)),
            _resident(wup.shape), _resident(cw.shape), _resident(cb.shape), _resident(wd.shape),
            _resident(bd.shape), _resident(g2.shape), _resident(b2.shape),
        ],
        out_specs=pl.BlockSpec((tm, D_MODEL), lambda i: (i, 0)),
        out_shape=jax.ShapeDtypeStruct((tokens, D_MODEL), F32),
        scratch_shapes=[
            pltpu.VMEM((2, FF_SLABS, tm + 8, LANES), F32),
            pltpu.VMEM((FF_STEPS, FF_SLABS, 8, LANES), F32),
            pltpu.VMEM((tm, D_MODEL), F32),
        ],
        compiler_params=pltpu.CompilerParams(
            dimension_semantics=("arbitrary",), vmem_limit_bytes=VMEM_LIMIT),
        name="conv_ffn",
    )(h2, wup, cw, cb, wd, bd, g2, b2)


def _tri_const():
    j = jnp.arange(Q_BLOCK)[:, None]
    s = jnp.arange(Q_BLOCK)[None, :]
    half = jnp.concatenate([-(j >= s).astype(F32), -jnp.ones((Q_BLOCK, Q_BLOCK), F32)], axis=1)
    return jnp.concatenate([half, half], axis=0).astype(BF16)


def _layer(h, w_in, b_gate, ln_sg_g, ln_sg_b, w_spatial, b_spatial, w_branch_a, w_branch_b, w_out,
           b_out, ln1_g, ln1_b, w_up, conv_w, conv_b, w_down, b_down, ln2_g, ln2_b):
    batch, seq, _ = h.shape
    tokens = batch * seq
    x2 = h.reshape(tokens, D_MODEL)
    row = lambda v: v.reshape(1, -1)

    wkt = w_in[:, SB_WIDTH:2 * SB_WIDTH].T
    bsp = jnp.repeat(b_spatial.T, SG_WIDTH // N_SG_GROUPS, axis=1)

    q2, kt3, v2 = _qkv_proj(x2, w_in, wkt, batch, seq, tm=QKV_TM)
    ya3 = _sb_attention(q2.reshape(batch, seq, SB_WIDTH), kt3, v2.reshape(batch, seq, SB_WIDTH),
                        _tri_const(), sub=ATTN_SUB, npair=ATTN_NPAIR)
    h1 = _mixer_out(x2, ya3.reshape(tokens, SB_WIDTH), w_in, row(b_gate), row(ln_sg_g),
                    row(ln_sg_b), w_spatial, bsp, w_branch_a.astype(BF16), w_branch_b, w_out,
                    row(b_out), row(ln1_g), row(ln1_b), tm=MIXER_TM)
    h2 = _conv_ffn(h1, w_up, conv_w, row(conv_b), w_down, row(b_down), row(ln2_g), row(ln2_b), seq,
                   tm=FFN_TM)
    return h2.reshape(batch, seq, D_MODEL)


def kernel(x, w_in, b_gate, ln_sg_g, ln_sg_b, w_spatial, b_spatial, w_branch_a, w_branch_b, w_out,
           b_out, ln1_g, ln1_b, w_up, conv_w, conv_b, w_down, b_down, ln2_g, ln2_b):
    h = x
    for l in range(w_in.shape[0]):
        h = _layer(h, w_in[l], b_gate[l], ln_sg_g[l], ln_sg_b[l], w_spatial[l], b_spatial[l],
                   w_branch_a[l], w_branch_b[l], w_out[l], b_out[l], ln1_g[l], ln1_b[l], w_up[l],
                   conv_w[l], conv_b[l], w_down[l], b_down[l], ln2_g[l], ln2_b[l])
    return h
```

```python
import functools

import jax
import jax.numpy as jnp
from jax import lax
from jax.experimental import pallas as pl
from jax.experimental.pallas import tpu as pltpu

F32 = jnp.float32
BF16 = jnp.bfloat16

D_MODEL = 1024
N_HEADS = 8
HEAD_DIM = 64
SB_WIDTH = N_HEADS * HEAD_DIM
SG_WIDTH = 512
COL_U = 3 * SB_WIDTH
COL_VSG = COL_U + SG_WIDTH
COL_GATE = COL_VSG + SG_WIDTH
N_SG_GROUPS = 8
CHUNK = 128
Q_BLOCK = 128
D_FF = 2816
LN_EPS = 1e-5
LOG2E = 1.4426950408889634
SCORE_SCALE = HEAD_DIM ** -0.5 * LOG2E
NEG_FILL = -1e30
STICK_CUTOFF = 64.0 * LOG2E
DN_ALPHA = 2.0 ** 0.25

LANES = 128
SUBLANES = 8
PAIR = 2 * HEAD_DIM
N_PAIRS = N_HEADS // 2
FF_PAIR = 256
FF_STEPS = D_FF // FF_PAIR
FF_SLABS = 2 * FF_PAIR // LANES

QKV_TM = 2048
ATTN_SUB = 8
ATTN_NPAIR = 2
MIXER_TM = 512
FFN_TM = 512
VMEM_LIMIT = 56 * 1024 * 1024


def _ln(x, g, b):
    mu = jnp.mean(x, axis=-1, keepdims=True)
    xc = x - mu
    var = jnp.mean(xc * xc, axis=-1, keepdims=True)
    return xc * lax.rsqrt(var + LN_EPS) * g + b


def _gelu_tanh(x):
    c = 0.7978845608028654
    return x * (0.5 * (1.0 + jnp.tanh(c * (x + 0.044715 * (x * x * x)))))


def _sigmoid(x):
    return 1.0 / (1.0 + jnp.exp(-x))


def _dot(a, b):
    return lax.dot_general(a, b, (((1,), (0,)), ((), ())), preferred_element_type=F32)


def _resident(shape):
    nd = len(shape)
    return pl.BlockSpec(shape, lambda *_: (0,) * nd, pipeline_mode=pl.Buffered(1))


def _qkv_kernel(x_ref, wq_ref, wkt_ref, wv_ref, q_ref, kt_ref, v_ref):
    x = x_ref[...].astype(BF16)
    q_ref[...] = (_dot(x, wq_ref[...]) * SCORE_SCALE).astype(BF16)
    v_ref[...] = _dot(x, wv_ref[...]).astype(BF16)
    kt = lax.dot_general(wkt_ref[...], x, (((1,), (1,)), ((), ())),
                         preferred_element_type=F32)
    kt_ref[0] = kt.astype(BF16)


def _qkv_proj(x2, w_in, wkt, batch, seq, tm):
    tokens = x2.shape[0]
    tiles_per_seq = seq // tm
    col_block = lambda k: pl.BlockSpec((D_MODEL, SB_WIDTH), lambda i: (0, k), pipeline_mode=pl.Buffered(1))
    return pl.pallas_call(
        _qkv_kernel,
        grid=(tokens // tm,),
        in_specs=[
            pl.BlockSpec((tm, D_MODEL), lambda i: (i, 0)),
            col_block(0),
            _resident((SB_WIDTH, D_MODEL)),
            col_block(2),
        ],
        out_specs=[
            pl.BlockSpec((tm, SB_WIDTH), lambda i: (i, 0)),
            pl.BlockSpec((1, SB_WIDTH, tm), lambda i: (i // tiles_per_seq, 0, i % tiles_per_seq)),
            pl.BlockSpec((tm, SB_WIDTH), lambda i: (i, 0)),
        ],
        out_shape=[
            jax.ShapeDtypeStruct((tokens, SB_WIDTH), BF16),
            jax.ShapeDtypeStruct((batch, SB_WIDTH, seq), BF16),
            jax.ShapeDtypeStruct((tokens, SB_WIDTH), BF16),
        ],
        compiler_params=pltpu.CompilerParams(
            dimension_semantics=("arbitrary",), vmem_limit_bytes=VMEM_LIMIT),
        name="qkv_proj",
    )(x2, w_in, wkt, w_in)


def _softplus2(z):
    return jnp.maximum(z, 0.0) + jnp.log(1.0 + jnp.exp2(-jnp.abs(z))) * LOG2E


def _sb_attn_kernel(q_ref, kt_ref, v_ref, tri_ref, o_ref, kbd_ref, vbd_ref, pre_ref, rs_ref,
                    carry_ref, acc_ref, *, n_kblocks, sub, npair):
    qs = pl.program_id(2)
    mq = sub * Q_BLOCK
    pairs = range(npair)

    @pl.when(qs == 0)
    def _():
        row_lo = lax.broadcasted_iota(jnp.int32, (PAIR, Q_BLOCK), 0) < HEAD_DIM
        lane_lo = lax.broadcasted_iota(jnp.int32, (Q_BLOCK, PAIR), 1) < HEAD_DIM
        zero = jnp.zeros((PAIR, Q_BLOCK), BF16)
        for p in pairs:
            for j in range(n_kblocks):
                kt = kt_ref[0, p * PAIR:(p + 1) * PAIR, j * Q_BLOCK:(j + 1) * Q_BLOCK]
                kbd_ref[p, j, :, 0:Q_BLOCK] = jnp.where(row_lo, kt, zero)
                kbd_ref[p, j, :, Q_BLOCK:2 * Q_BLOCK] = jnp.where(row_lo, zero, kt)
                vt = v_ref[0, j * Q_BLOCK:(j + 1) * Q_BLOCK, p * PAIR:(p + 1) * PAIR]
                vbd_ref[p, j, 0:Q_BLOCK, :] = jnp.where(lane_lo, vt, zero)
                vbd_ref[p, j, Q_BLOCK:2 * Q_BLOCK, :] = jnp.where(lane_lo, zero, vt)
            vbd_ref[p, n_kblocks] = jnp.zeros((2 * Q_BLOCK, PAIR), BF16)

    tri = tri_ref[...]
    row = lax.broadcasted_iota(jnp.int32, (Q_BLOCK, 2 * Q_BLOCK), 0)
    col = lax.broadcasted_iota(jnp.int32, (Q_BLOCK, 2 * Q_BLOCK), 1) & (Q_BLOCK - 1)
    causal = col < row

    def mask_top(x, fill, diag):
        if not diag:
            return x
        top = jnp.where(causal, x[:Q_BLOCK], fill)
        return top if x.shape[0] == Q_BLOCK else jnp.concatenate([top, x[Q_BLOCK:]], axis=0)

    def stage1(p, r0, r1, j, diag):
        q2 = q_ref[0, r0:r1, p * PAIR:(p + 1) * PAIR]
        z = _dot(q2, kbd_ref[p, j])
        sp = mask_top(_softplus2(z), 0.0, diag)
        hi = sp.astype(BF16)
        lo = (sp - hi.astype(F32)).astype(BF16)
        cr = [_dot(jnp.concatenate([hi[:, h * Q_BLOCK:(h + 1) * Q_BLOCK],
                                    lo[:, h * Q_BLOCK:(h + 1) * Q_BLOCK]], axis=1), tri)
              for h in range(2)]
        pre = z + jnp.concatenate([cr[0][:, :Q_BLOCK], cr[1][:, :Q_BLOCK]], axis=1)
        rs = jnp.concatenate([cr[0][:, Q_BLOCK:], cr[1][:, Q_BLOCK:]], axis=1)
        return pre, rs

    def stage2(p, pre, rs, r0, r1, j, diag, jv=None):
        c = carry_ref[p, r0:r1, :]
        c_new = c + rs
        carry_ref[p, r0:r1, :] = c_new
        w = jnp.exp2(mask_top(pre + c, NEG_FILL, diag)).astype(BF16)
        acc_ref[p, r0:r1, :] += _dot(w, vbd_ref[p, j if jv is None else jv])
        return jnp.max(c_new[-Q_BLOCK:])

    def block(p, r0, r1, j, diag, jv=None):
        pre, rs = stage1(p, r0, r1, j, diag)
        return stage2(p, pre, rs, r0, r1, j, diag, jv)

    def stick_left():
        return functools.reduce(jnp.maximum, [jnp.max(carry_ref[p]) for p in pairs])

    carry_ref[...] = jnp.zeros_like(carry_ref)
    acc_ref[...] = jnp.zeros_like(acc_ref)
    j0 = qs * sub
    j1 = jnp.maximum(j0 - 1, 0)
    jv = jnp.where(qs > 0, j1, n_kblocks)

    stick = []
    for d in reversed(range(-1, sub)):
        r0, r1 = max(d, 0) * Q_BLOCK, min(d + 2, sub) * Q_BLOCK
        for p in pairs:
            left = block(p, r0, r1, j0 + d, True) if d >= 0 else block(p, r0, r1, j1, False, jv)
            if d < sub - 1:
                stick.append(left)

    @pl.when(functools.reduce(jnp.maximum, stick) >= -STICK_CUTOFF)
    def _():
        for d in reversed(range(0, sub - 2)):
            for p in pairs:
                block(p, (d + 2) * Q_BLOCK, mq, j0 + d, False)

        @pl.when(qs > 0)
        def _():
            for p in pairs:
                block(p, Q_BLOCK, mq, j0 - 1, False)

            @pl.when(jnp.logical_and(j0 >= 2, stick_left() >= -STICK_CUTOFF))
            def _():
                for p in pairs:
                    pre, rs = stage1(p, 0, mq, j0 - 2, False)
                    pre_ref[p] = pre
                    rs_ref[p] = rs

                def cond(state):
                    j, alive = state
                    return jnp.logical_and(j >= 1, alive > 0)

                def body(state):
                    j, _ = state
                    for p in pairs:
                        stage2(p, pre_ref[p], rs_ref[p], 0, mq, j, False)
                        pre_n, rs_n = stage1(p, 0, mq, j - 1, False)
                        pre_ref[p] = pre_n
                        rs_ref[p] = rs_n
                    return j - 1, (stick_left() >= -STICK_CUTOFF).astype(jnp.int32)

                j_end, alive = lax.while_loop(cond, body, (j0 - 2, jnp.int32(1)))

                @pl.when(jnp.logical_and(j_end == 0, alive > 0))
                def _():
                    for p in pairs:
                        stage2(p, pre_ref[p], rs_ref[p], 0, mq, 0, False)

    for p in pairs:
        o_ref[0, :, p * PAIR:(p + 1) * PAIR] = acc_ref[p].astype(BF16)


def _sb_attention(q3, kt3, v3, tri, sub, npair):
    batch, seq, _ = q3.shape
    n_blocks = seq // Q_BLOCK
    mq = sub * Q_BLOCK
    width = npair * PAIR
    return pl.pallas_call(
        functools.partial(_sb_attn_kernel, n_kblocks=n_blocks, sub=sub, npair=npair),
        grid=(batch, N_PAIRS // npair, seq // mq),
        in_specs=[
            pl.BlockSpec((1, mq, width), lambda b, g, i: (b, i, g)),
            pl.BlockSpec((1, width, seq), lambda b, g, i: (b, g, 0)),
            pl.BlockSpec((1, seq, width), lambda b, g, i: (b, 0, g)),
            _resident((2 * Q_BLOCK, 2 * Q_BLOCK)),
        ],
        out_specs=pl.BlockSpec((1, mq, width), lambda b, g, i: (b, i, g)),
        out_shape=jax.ShapeDtypeStruct((batch, seq, SB_WIDTH), BF16),
        scratch_shapes=[
            pltpu.VMEM((npair, n_blocks, PAIR, 2 * Q_BLOCK), BF16),
            pltpu.VMEM((npair, n_blocks + 1, 2 * Q_BLOCK, PAIR), BF16),
            pltpu.VMEM((npair, mq, 2 * Q_BLOCK), F32),
            pltpu.VMEM((npair, mq, 2 * Q_BLOCK), F32),
            pltpu.VMEM((npair, mq, 2 * Q_BLOCK), F32),
            pltpu.VMEM((npair, mq, PAIR), F32),
        ],
        compiler_params=pltpu.CompilerParams(
            dimension_semantics=("arbitrary", "arbitrary", "arbitrary"),
            vmem_limit_bytes=VMEM_LIMIT),
        name="sb_attn",
    )(q3, kt3, v3, tri)


def _mixer_out_kernel(x_ref, ya_ref, win_ref, bg_ref, lsg_ref, lsb_ref, wsp_ref,
                      bsp_ref, wa_ref, wb_ref, wo_ref, bo_ref, g1_ref, b1_ref, o_ref):
    x = x_ref[...]
    xb = x.astype(BF16)
    tm = x.shape[0]

    u = _gelu_tanh(_dot(xb, win_ref[:, COL_U:COL_VSG]))
    vs = _ln(_gelu_tanh(_dot(xb, win_ref[:, COL_VSG:COL_GATE])), lsg_ref[...], lsb_ref[...]).astype(BF16)

    tril = (lax.broadcasted_iota(jnp.int32, (CHUNK, CHUNK), 1)
            <= lax.broadcasted_iota(jnp.int32, (CHUNK, CHUNK), 0))
    wsp = [jnp.where(tril, wsp_ref[g], 0.0).astype(BF16) for g in range(N_SG_GROUPS)]
    lane_lo = lax.broadcasted_iota(jnp.int32, (CHUNK, LANES), 1) < HEAD_DIM
    zero = jnp.zeros((CHUNK, LANES), BF16)
    rows = []
    for c in range(tm // CHUNK):
        parts = []
        for p in range(N_SG_GROUPS // 2):
            v2 = vs[c * CHUNK:(c + 1) * CHUNK, p * LANES:(p + 1) * LANES]
            parts.append(_dot(wsp[2 * p], jnp.where(lane_lo, v2, zero))
                         + _dot(wsp[2 * p + 1], jnp.where(lane_lo, zero, v2)))
        rows.append(jnp.concatenate(parts, axis=1) + bsp_ref[...])
    yb = (u * jnp.concatenate(rows, axis=0)).astype(BF16)

    g = _sigmoid(_dot(xb, win_ref[:, COL_GATE:]) + bg_ref[...])
    merged = (g[:, :D_MODEL] * _dot(ya_ref[...], wa_ref[...])
              + g[:, D_MODEL:] * _dot(yb, wb_ref[...]))
    y = _dot(merged.astype(BF16), wo_ref[...]) + bo_ref[...]
    o_ref[...] = _ln(DN_ALPHA * x + y, g1_ref[...], b1_ref[...])


def _mixer_out(x2, ya2, w_in, bg, lsg, lsb, wsp, bsp, wa, wb, wo, bo, g1, b1, tm):
    tokens = x2.shape[0]
    return pl.pallas_call(
        _mixer_out_kernel,
        grid=(tokens // tm,),
        in_specs=[
            pl.BlockSpec((tm, D_MODEL), lambda i: (i, 0)),
            pl.BlockSpec((tm, SB_WIDTH), lambda i: (i, 0)),
            _resident(w_in.shape), _resident(bg.shape),
            _resident(lsg.shape), _resident(lsb.shape), _resident(wsp.shape), _resident(bsp.shape),
            _resident(wa.shape), _resident(wb.shape), _resident(wo.shape), _resident(bo.shape),
            _resident(g1.shape), _resident(b1.shape),
        ],
        out_specs=pl.BlockSpec((tm, D_MODEL), lambda i: (i, 0)),
        out_shape=jax.ShapeDtypeStruct((tokens, D_MODEL), F32),
        compiler_params=pltpu.CompilerParams(
            dimension_semantics=("arbitrary",), vmem_limit_bytes=VMEM_LIMIT),
        name="mixer_out",
    )(x2, ya2, w_in, bg, lsg, lsb, wsp, bsp, wa, wb, wo, bo, g1, b1)


def _conv_ffn_kernel(h_ref, wup_ref, cw_ref, cb_ref, wd_ref, bd_ref, g2_ref, b2_ref, o_ref,
                     slab_ref, tail_ref, acc_ref, *, tiles_per_seq):
    tm = h_ref.shape[0]
    half = FF_SLABS // 2
    PRE = SUBLANES
    seq_start = (pl.program_id(0) % tiles_per_seq) == 0
    hb = h_ref[...].astype(BF16)

    def produce(c, slot):
        for part in range(2):
            c0 = part * D_FF + c * FF_PAIR
            up = _dot(hb, wup_ref[:, c0:c0 + FF_PAIR])
            for s in range(half):
                slab_ref[slot, part * half + s, PRE:tm + PRE, :] = up[:, s * LANES:(s + 1) * LANES]

    def consume(c, slot):
        conv = []
        for s in range(FF_SLABS):
            c0 = (s // half) * D_FF + c * FF_PAIR + (s % half) * LANES
            cw = cw_ref[:, c0:c0 + LANES]
            slab_ref[slot, s, 0:PRE, :] = jnp.where(seq_start, 0.0, tail_ref[c, s])
            tail_ref[c, s] = slab_ref[slot, s, tm:tm + PRE, :]
            conv.append(cb_ref[:, c0:c0 + LANES]
                        + cw[0:1] * slab_ref[slot, s, PRE - 2:tm + PRE - 2, :]
                        + cw[1:2] * slab_ref[slot, s, PRE - 1:tm + PRE - 1, :]
                        + cw[2:3] * slab_ref[slot, s, PRE:tm + PRE, :])
        act = jnp.concatenate([(conv[s] * _sigmoid(conv[s]) * conv[half + s]).astype(BF16)
                               for s in range(half)], axis=1)
        acc_ref[...] += _dot(act, wd_ref[c * FF_PAIR:(c + 1) * FF_PAIR, :])

    acc_ref[...] = DN_ALPHA * h_ref[...] + bd_ref[...]
    produce(0, 0)
    for c in range(FF_STEPS):
        if c + 1 < FF_STEPS:
            produce(c + 1, (c + 1) % 2)
        consume(c, c % 2)
    o_ref[...] = _ln(acc_ref[...], g2_ref[...], b2_ref[...])


def _conv_ffn(h2, wup, cw, cb, wd, bd, g2, b2, seq, tm):
    tokens = h2.shape[0]
    return pl.pallas_call(
        functools.partial(_conv_ffn_kernel, tiles_per_seq=seq // tm),
        grid=(tokens // tm,),
        in_specs=[
            pl.BlockSpec((tm, D_MODEL), lambda i: (i, 0)),
            _resident(wup.shape), _resident(cw.shape), _resident(cb.shape), _resident(wd.shape),
            _resident(bd.shape), _resident(g2.shape), _resident(b2.shape),
        ],
        out_specs=pl.BlockSpec((tm, D_MODEL), lambda i: (i, 0)),
        out_shape=jax.ShapeDtypeStruct((tokens, D_MODEL), F32),
        scratch_shapes=[
            pltpu.VMEM((2, FF_SLABS, tm + SUBLANES, LANES), F32),
            pltpu.VMEM((FF_STEPS, FF_SLABS, SUBLANES, LANES), F32),
            pltpu.VMEM((tm, D_MODEL), F32),
        ],
        compiler_params=pltpu.CompilerParams(
            dimension_semantics=("arbitrary",), vmem_limit_bytes=VMEM_LIMIT),
        name="conv_ffn",
    )(h2, wup, cw, cb, wd, bd, g2, b2)


def _tri_const():
    j = jnp.arange(Q_BLOCK)[:, None]
    s = jnp.arange(Q_BLOCK)[None, :]
    half = jnp.concatenate([-(j >= s).astype(F32), -jnp.ones((Q_BLOCK, Q_BLOCK), F32)], axis=1)
    return jnp.concatenate([half, half], axis=0).astype(BF16)


def _layer(h, w_in, b_gate, ln_sg_g, ln_sg_b, w_spatial, b_spatial, w_branch_a, w_branch_b, w_out,
           b_out, ln1_g, ln1_b, w_up, conv_w, conv_b, w_down, b_down, ln2_g, ln2_b):
    batch, seq, _ = h.shape
    tokens = batch * seq
    x2 = h.reshape(tokens, D_MODEL)
    row = lambda v: v.reshape(1, -1)

    wkt = w_in[:, SB_WIDTH:2 * SB_WIDTH].T
    bsp = jnp.repeat(b_spatial.T, SG_WIDTH // N_SG_GROUPS, axis=1)

    q2, kt3, v2 = _qkv_proj(x2, w_in, wkt, batch, seq, tm=QKV_TM)
    ya3 = _sb_attention(q2.reshape(batch, seq, SB_WIDTH), kt3, v2.reshape(batch, seq, SB_WIDTH),
                        _tri_const(), sub=ATTN_SUB, npair=ATTN_NPAIR)
    h1 = _mixer_out(x2, ya3.reshape(tokens, SB_WIDTH), w_in, row(b_gate), row(ln_sg_g),
                    row(ln_sg_b), w_spatial, bsp, w_branch_a, w_branch_b, w_out,
                    row(b_out), row(ln1_g), row(ln1_b), tm=MIXER_TM)
    h2 = _conv_ffn(h1, w_up, conv_w, row(conv_b), w_down, row(b_down), row(ln2_g), row(ln2_b), seq,
                   tm=FFN_TM)
    return h2.reshape(batch, seq, D_MODEL)


def kernel(x, w_in, b_gate, ln_sg_g, ln_sg_b, w_spatial, b_spatial, w_branch_a, w_branch_b, w_out,
           b_out, ln1_g, ln1_b, w_up, conv_w, conv_b, w_down, b_down, ln2_g, ln2_b):
    h = x
    for l in range(w_in.shape[0]):
        h = _layer(h, w_in[l], b_gate[l], ln_sg_g[l], ln_sg_b[l], w_spatial[l], b_spatial[l],
                   w_branch_a[l], w_branch_b[l], w_out[l], b_out[l], ln1_g[l], ln1_b[l], w_up[l],
                   conv_w[l], conv_b[l], w_down[l], b_down[l], ln2_g[l], ln2_b[l])
    return h
```

```python
import functools

import jax
import jax.numpy as jnp
from jax import lax
from jax.experimental import pallas as pl
from jax.experimental.pallas import tpu as pltpu

F32 = jnp.float32
BF16 = jnp.bfloat16

D_MODEL = 1024
N_HEADS = 8
HEAD_DIM = 64
SB_WIDTH = N_HEADS * HEAD_DIM
SG_WIDTH = 512
COL_U = 3 * SB_WIDTH
COL_VSG = COL_U + SG_WIDTH
COL_GATE = COL_VSG + SG_WIDTH
N_SG_GROUPS = 8
CHUNK = 128
Q_BLOCK = 128
D_FF = 2816
LN_EPS = 1e-5
LOG2E = 1.4426950408889634
SCORE_SCALE = HEAD_DIM ** -0.5 * LOG2E
NEG_FILL = -1e30
STICK_CUTOFF = 64.0 * LOG2E
DN_ALPHA = 2.0 ** 0.25

LANES = 128
SUBLANES = 8
PAIR = 2 * HEAD_DIM
N_PAIRS = N_HEADS // 2
FF_PAIR = 256
FF_STEPS = D_FF // FF_PAIR
FF_SLABS = 2 * FF_PAIR // LANES

QKV_TM = 2048
ATTN_SUB = 8
ATTN_NPAIR = 2
MIXER_TM = 1024
FFN_TM = 512
VMEM_LIMIT = 56 * 1024 * 1024


def _ln(x, g, b):
    mu = jnp.mean(x, axis=-1, keepdims=True)
    xc = x - mu
    var = jnp.mean(xc * xc, axis=-1, keepdims=True)
    return xc * lax.rsqrt(var + LN_EPS) * g + b


def _gelu_tanh(x):
    c = 0.7978845608028654
    return x * (0.5 * (1.0 + jnp.tanh(c * (x + 0.044715 * (x * x * x)))))


def _sigmoid(x):
    return 1.0 / (1.0 + jnp.exp(-x))


def _dot(a, b):
    return lax.dot_general(a, b, (((1,), (0,)), ((), ())), preferred_element_type=F32)


def _resident(shape):
    nd = len(shape)
    return pl.BlockSpec(shape, lambda *_: (0,) * nd, pipeline_mode=pl.Buffered(1))


def _qkv_kernel(x_ref, wq_ref, wkt_ref, wv_ref, q_ref, kt_ref, v_ref):
    x = x_ref[...].astype(BF16)
    q_ref[...] = (_dot(x, wq_ref[...]) * SCORE_SCALE).astype(BF16)
    v_ref[...] = _dot(x, wv_ref[...]).astype(BF16)
    kt = lax.dot_general(wkt_ref[...], x, (((1,), (1,)), ((), ())),
                         preferred_element_type=F32)
    kt_ref[0] = kt.astype(BF16)


def _qkv_proj(x2, w_in, wkt, batch, seq, tm):
    tokens = x2.shape[0]
    tiles_per_seq = seq // tm
    col_block = lambda k: pl.BlockSpec((D_MODEL, SB_WIDTH), lambda i: (0, k), pipeline_mode=pl.Buffered(1))
    return pl.pallas_call(
        _qkv_kernel,
        grid=(tokens // tm,),
        in_specs=[
            pl.BlockSpec((tm, D_MODEL), lambda i: (i, 0)),
            col_block(0),
            _resident((SB_WIDTH, D_MODEL)),
            col_block(2),
        ],
        out_specs=[
            pl.BlockSpec((tm, SB_WIDTH), lambda i: (i, 0)),
            pl.BlockSpec((1, SB_WIDTH, tm), lambda i: (i // tiles_per_seq, 0, i % tiles_per_seq)),
            pl.BlockSpec((tm, SB_WIDTH), lambda i: (i, 0)),
        ],
        out_shape=[
            jax.ShapeDtypeStruct((tokens, SB_WIDTH), BF16),
            jax.ShapeDtypeStruct((batch, SB_WIDTH, seq), BF16),
            jax.ShapeDtypeStruct((tokens, SB_WIDTH), BF16),
        ],
        compiler_params=pltpu.CompilerParams(
            dimension_semantics=("arbitrary",), vmem_limit_bytes=VMEM_LIMIT),
        name="qkv_proj",
    )(x2, w_in, wkt, w_in)


def _softplus2(z):
    return jnp.maximum(z, 0.0) + jnp.log(1.0 + jnp.exp2(-jnp.abs(z))) * LOG2E


def _sb_attn_kernel(q_ref, kt_ref, v_ref, tri_ref, o_ref, kbd_ref, vbd_ref, pre_ref, rs_ref,
                    carry_ref, acc_ref, *, n_kblocks, sub, npair):
    qs = pl.program_id(2)
    mq = sub * Q_BLOCK
    pairs = range(npair)

    @pl.when(qs == 0)
    def _():
        row_lo = lax.broadcasted_iota(jnp.int32, (PAIR, Q_BLOCK), 0) < HEAD_DIM
        lane_lo = lax.broadcasted_iota(jnp.int32, (Q_BLOCK, PAIR), 1) < HEAD_DIM
        zero = jnp.zeros((PAIR, Q_BLOCK), BF16)
        for p in pairs:
            for j in range(n_kblocks):
                kt = kt_ref[0, p * PAIR:(p + 1) * PAIR, j * Q_BLOCK:(j + 1) * Q_BLOCK]
                kbd_ref[p, j, :, 0:Q_BLOCK] = jnp.where(row_lo, kt, zero)
                kbd_ref[p, j, :, Q_BLOCK:2 * Q_BLOCK] = jnp.where(row_lo, zero, kt)
                vt = v_ref[0, j * Q_BLOCK:(j + 1) * Q_BLOCK, p * PAIR:(p + 1) * PAIR]
                vbd_ref[p, j, 0:Q_BLOCK, :] = jnp.where(lane_lo, vt, zero)
                vbd_ref[p, j, Q_BLOCK:2 * Q_BLOCK, :] = jnp.where(lane_lo, zero, vt)
            vbd_ref[p, n_kblocks] = jnp.zeros((2 * Q_BLOCK, PAIR), BF16)

    tri = tri_ref[...]
    row = lax.broadcasted_iota(jnp.int32, (Q_BLOCK, 2 * Q_BLOCK), 0)
    col = lax.broadcasted_iota(jnp.int32, (Q_BLOCK, 2 * Q_BLOCK), 1) & (Q_BLOCK - 1)
    causal = col < row

    def mask_top(x, fill, diag):
        if not diag:
            return x
        top = jnp.where(causal, x[:Q_BLOCK], fill)
        return top if x.shape[0] == Q_BLOCK else jnp.concatenate([top, x[Q_BLOCK:]], axis=0)

    def stage1(p, r0, r1, j, diag):
        q2 = q_ref[0, r0:r1, p * PAIR:(p + 1) * PAIR]
        z = _dot(q2, kbd_ref[p, j])
        sp = mask_top(_softplus2(z), 0.0, diag)
        hi = sp.astype(BF16)
        lo = (sp - hi.astype(F32)).astype(BF16)
        cr = [_dot(jnp.concatenate([hi[:, h * Q_BLOCK:(h + 1) * Q_BLOCK],
                                    lo[:, h * Q_BLOCK:(h + 1) * Q_BLOCK]], axis=1), tri)
              for h in range(2)]
        pre = z + jnp.concatenate([cr[0][:, :Q_BLOCK], cr[1][:, :Q_BLOCK]], axis=1)
        rs = jnp.concatenate([cr[0][:, Q_BLOCK:], cr[1][:, Q_BLOCK:]], axis=1)
        return pre, rs

    def stage2(p, pre, rs, r0, r1, j, diag, jv=None):
        c = carry_ref[p, r0:r1, :]
        c_new = c + rs
        carry_ref[p, r0:r1, :] = c_new
        w = jnp.exp2(mask_top(pre + c, NEG_FILL, diag)).astype(BF16)
        acc_ref[p, r0:r1, :] += _dot(w, vbd_ref[p, j if jv is None else jv])
        return jnp.max(c_new[-Q_BLOCK:])

    def block(p, r0, r1, j, diag, jv=None):
        pre, rs = stage1(p, r0, r1, j, diag)
        return stage2(p, pre, rs, r0, r1, j, diag, jv)

    def stick_left():
        return functools.reduce(jnp.maximum, [jnp.max(carry_ref[p]) for p in pairs])

    carry_ref[...] = jnp.zeros_like(carry_ref)
    acc_ref[...] = jnp.zeros_like(acc_ref)
    j0 = qs * sub
    j1 = jnp.maximum(j0 - 1, 0)
    jv = jnp.where(qs > 0, j1, n_kblocks)

    stick = []
    for d in reversed(range(-1, sub)):
        r0, r1 = max(d, 0) * Q_BLOCK, min(d + 2, sub) * Q_BLOCK
        for p in pairs:
            left = block(p, r0, r1, j0 + d, True) if d >= 0 else block(p, r0, r1, j1, False, jv)
            if d < sub - 1:
                stick.append(left)

    @pl.when(functools.reduce(jnp.maximum, stick) >= -STICK_CUTOFF)
    def _():
        for d in reversed(range(0, sub - 2)):
            for p in pairs:
                block(p, (d + 2) * Q_BLOCK, mq, j0 + d, False)

        @pl.when(qs > 0)
        def _():
            for p in pairs:
                block(p, Q_BLOCK, mq, j0 - 1, False)

            @pl.when(jnp.logical_and(j0 >= 2, stick_left() >= -STICK_CUTOFF))
            def _():
                for p in pairs:
                    pre, rs = stage1(p, 0, mq, j0 - 2, False)
                    pre_ref[p] = pre
                    rs_ref[p] = rs

                def cond(state):
                    j, alive = state
                    return jnp.logical_and(j >= 1, alive > 0)

                def body(state):
                    j, _ = state
                    for p in pairs:
                        stage2(p, pre_ref[p], rs_ref[p], 0, mq, j, False)
                        pre_n, rs_n = stage1(p, 0, mq, j - 1, False)
                        pre_ref[p] = pre_n
                        rs_ref[p] = rs_n
                    return j - 1, (stick_left() >= -STICK_CUTOFF).astype(jnp.int32)

                j_end, alive = lax.while_loop(cond, body, (j0 - 2, jnp.int32(1)))

                @pl.when(jnp.logical_and(j_end == 0, alive > 0))
                def _():
                    for p in pairs:
                        stage2(p, pre_ref[p], rs_ref[p], 0, mq, 0, False)

    for p in pairs:
        o_ref[0, :, p * PAIR:(p + 1) * PAIR] = acc_ref[p].astype(BF16)


def _sb_attention(q3, kt3, v3, tri, sub, npair):
    batch, seq, _ = q3.shape
    n_blocks = seq // Q_BLOCK
    mq = sub * Q_BLOCK
    width = npair * PAIR
    return pl.pallas_call(
        functools.partial(_sb_attn_kernel, n_kblocks=n_blocks, sub=sub, npair=npair),
        grid=(batch, N_PAIRS // npair, seq // mq),
        in_specs=[
            pl.BlockSpec((1, mq, width), lambda b, g, i: (b, i, g)),
            pl.BlockSpec((1, width, seq), lambda b, g, i: (b, g, 0)),
            pl.BlockSpec((1, seq, width), lambda b, g, i: (b, 0, g)),
            _resident((2 * Q_BLOCK, 2 * Q_BLOCK)),
        ],
        out_specs=pl.BlockSpec((1, mq, width), lambda b, g, i: (b, i, g)),
        out_shape=jax.ShapeDtypeStruct((batch, seq, SB_WIDTH), BF16),
        scratch_shapes=[
            pltpu.VMEM((npair, n_blocks, PAIR, 2 * Q_BLOCK), BF16),
            pltpu.VMEM((npair, n_blocks + 1, 2 * Q_BLOCK, PAIR), BF16),
            pltpu.VMEM((npair, mq, 2 * Q_BLOCK), F32),
            pltpu.VMEM((npair, mq, 2 * Q_BLOCK), F32),
            pltpu.VMEM((npair, mq, 2 * Q_BLOCK), F32),
            pltpu.VMEM((npair, mq, PAIR), F32),
        ],
        compiler_params=pltpu.CompilerParams(
            dimension_semantics=("arbitrary", "arbitrary", "arbitrary"),
            vmem_limit_bytes=VMEM_LIMIT),
        name="sb_attn",
    )(q3, kt3, v3, tri)


def _mixer_out_kernel(x_ref, ya_ref, wu_ref, wvs_ref, wg0_ref, wg1_ref, wg2_ref, wg3_ref, bg_ref, lsg_ref,
                      lsb_ref, wsp_ref, bsp_ref, wa_ref, wb_ref, wo_ref, bo_ref, g1_ref, b1_ref, o_ref):
    x = x_ref[...]
    xb = x.astype(BF16)
    tm = x.shape[0]

    u = _gelu_tanh(_dot(xb, wu_ref[...]))
    vs = _ln(_gelu_tanh(_dot(xb, wvs_ref[...])), lsg_ref[...], lsb_ref[...]).astype(BF16)

    tril = (lax.broadcasted_iota(jnp.int32, (CHUNK, CHUNK), 1)
            <= lax.broadcasted_iota(jnp.int32, (CHUNK, CHUNK), 0))
    wsp = [jnp.where(tril, wsp_ref[g], 0.0).astype(BF16) for g in range(N_SG_GROUPS)]
    lane_lo = lax.broadcasted_iota(jnp.int32, (CHUNK, LANES), 1) < HEAD_DIM
    zero = jnp.zeros((CHUNK, LANES), BF16)
    rows = []
    for c in range(tm // CHUNK):
        parts = []
        for p in range(N_SG_GROUPS // 2):
            v2 = vs[c * CHUNK:(c + 1) * CHUNK, p * LANES:(p + 1) * LANES]
            parts.append(_dot(wsp[2 * p], jnp.where(lane_lo, v2, zero))
                         + _dot(wsp[2 * p + 1], jnp.where(lane_lo, zero, v2)))
        rows.append(jnp.concatenate(parts, axis=1) + bsp_ref[...])
    yb = (u * jnp.concatenate(rows, axis=0)).astype(BF16)

    ga = _sigmoid(jnp.concatenate([_dot(xb, wg0_ref[...]), _dot(xb, wg1_ref[...])], axis=1) + bg_ref[:, :D_MODEL])
    gb = _sigmoid(jnp.concatenate([_dot(xb, wg2_ref[...]), _dot(xb, wg3_ref[...])], axis=1) + bg_ref[:, D_MODEL:])
    merged = ga * _dot(ya_ref[...], wa_ref[...]) + gb * _dot(yb, wb_ref[...])
    y = _dot(merged.astype(BF16), wo_ref[...]) + bo_ref[...]
    o_ref[...] = _ln(DN_ALPHA * x + y, g1_ref[...], b1_ref[...])


def _mixer_out(x2, ya2, w_in, bg, lsg, lsb, wsp, bsp, wa, wb, wo, bo, g1, b1, tm):
    tokens = x2.shape[0]
    col_block = lambda k: pl.BlockSpec((D_MODEL, SB_WIDTH), lambda i: (0, k), pipeline_mode=pl.Buffered(1))
    return pl.pallas_call(
        _mixer_out_kernel,
        grid=(tokens // tm,),
        in_specs=[
            pl.BlockSpec((tm, D_MODEL), lambda i: (i, 0)),
            pl.BlockSpec((tm, SB_WIDTH), lambda i: (i, 0)),
            col_block(COL_U // SB_WIDTH), col_block(COL_VSG // SB_WIDTH),
            *[col_block(COL_GATE // SB_WIDTH + k) for k in range(2 * D_MODEL // SB_WIDTH)],
            _resident(bg.shape),
            _resident(lsg.shape), _resident(lsb.shape), _resident(wsp.shape), _resident(bsp.shape),
            _resident(wa.shape), _resident(wb.shape), _resident(wo.shape), _resident(bo.shape),
            _resident(g1.shape), _resident(b1.shape),
        ],
        out_specs=pl.BlockSpec((tm, D_MODEL), lambda i: (i, 0)),
        out_shape=jax.ShapeDtypeStruct((tokens, D_MODEL), F32),
        compiler_params=pltpu.CompilerParams(
            dimension_semantics=("arbitrary",), vmem_limit_bytes=VMEM_LIMIT),
        name="mixer_out",
    )(x2, ya2, w_in, w_in, w_in, w_in, w_in, w_in, bg, lsg, lsb, wsp, bsp, wa, wb, wo, bo, g1, b1)


def _conv_ffn_kernel(h_ref, wup_ref, cw_ref, cb_ref, wd_ref, bd_ref, g2_ref, b2_ref, o_ref,
                     slab_ref, tail_ref, acc_ref, *, tiles_per_seq):
    tm = h_ref.shape[0]
    half = FF_SLABS // 2
    PRE = SUBLANES
    seq_start = (pl.program_id(0) % tiles_per_seq) == 0
    hb = h_ref[...].astype(BF16)

    def produce(c, slot):
        for part in range(2):
            c0 = part * D_FF + c * FF_PAIR
            up = _dot(hb, wup_ref[:, c0:c0 + FF_PAIR])
            for s in range(half):
                slab_ref[slot, part * half + s, PRE:tm + PRE, :] = up[:, s * LANES:(s + 1) * LANES]

    def consume(c, slot):
        conv = []
        for s in range(FF_SLABS):
            c0 = (s // half) * D_FF + c * FF_PAIR + (s % half) * LANES
            cw = cw_ref[:, c0:c0 + LANES]
            slab_ref[slot, s, 0:PRE, :] = jnp.where(seq_start, 0.0, tail_ref[c, s])
            tail_ref[c, s] = slab_ref[slot, s, tm:tm + PRE, :]
            conv.append(cb_ref[:, c0:c0 + LANES]
                        + cw[0:1] * slab_ref[slot, s, PRE - 2:tm + PRE - 2, :]
                        + cw[1:2] * slab_ref[slot, s, PRE - 1:tm + PRE - 1, :]
                        + cw[2:3] * slab_ref[slot, s, PRE:tm + PRE, :])
        act = jnp.concatenate([(conv[s] * _sigmoid(conv[s]) * conv[half + s]).astype(BF16)
                               for s in range(half)], axis=1)
        acc_ref[...] += _dot(act, wd_ref[c * FF_PAIR:(c + 1) * FF_PAIR, :])

    acc_ref[...] = DN_ALPHA * h_ref[...] + bd_ref[...]
    produce(0, 0)
    for c in range(FF_STEPS):
        if c + 1 < FF_STEPS:
            produce(c + 1, (c + 1) % 2)
        consume(c, c % 2)
    o_ref[...] = _ln(acc_ref[...], g2_ref[...], b2_ref[...])


def _conv_ffn(h2, wup, cw, cb, wd, bd, g2, b2, seq, tm):
    tokens = h2.shape[0]
    return pl.pallas_call(
        functools.partial(_conv_ffn_kernel, tiles_per_seq=seq // tm),
        grid=(tokens // tm,),
        in_specs=[
            pl.BlockSpec((tm, D_MODEL), lambda i: (i, 0)),
            _resident(wup.shape), _resident(cw.shape), _resident(cb.shape), _resident(wd.shape),
            _resident(bd.shape), _resident(g2.shape), _resident(b2.shape),
        ],
        out_specs=pl.BlockSpec((tm, D_MODEL), lambda i: (i, 0)),
        out_shape=jax.ShapeDtypeStruct((tokens, D_MODEL), F32),
        scratch_shapes=[
            pltpu.VMEM((2, FF_SLABS, tm + SUBLANES, LANES), F32),
            pltpu.VMEM((FF_STEPS, FF_SLABS, SUBLANES, LANES), F32),
            pltpu.VMEM((tm, D_MODEL), F32),
        ],
        compiler_params=pltpu.CompilerParams(
            dimension_semantics=("arbitrary",), vmem_limit_bytes=VMEM_LIMIT),
        name="conv_ffn",
    )(h2, wup, cw, cb, wd, bd, g2, b2)


def _tri_const():
    j = jnp.arange(Q_BLOCK)[:, None]
    s = jnp.arange(Q_BLOCK)[None, :]
    half = jnp.concatenate([-(j >= s).astype(F32), -jnp.ones((Q_BLOCK, Q_BLOCK), F32)], axis=1)
    return jnp.concatenate([half, half], axis=0).astype(BF16)


def _layer(h, w_in, b_gate, ln_sg_g, ln_sg_b, w_spatial, b_spatial, w_branch_a, w_branch_b, w_out,
           b_out, ln1_g, ln1_b, w_up, conv_w, conv_b, w_down, b_down, ln2_g, ln2_b):
    batch, seq, _ = h.shape
    tokens = batch * seq
    x2 = h.reshape(tokens, D_MODEL)
    row = lambda v: v.reshape(1, -1)

    wkt = w_in[:, SB_WIDTH:2 * SB_WIDTH].T
    bsp = jnp.repeat(b_spatial.T, SG_WIDTH // N_SG_GROUPS, axis=1)

    q2, kt3, v2 = _qkv_proj(x2, w_in, wkt, batch, seq, tm=QKV_TM)
    ya3 = _sb_attention(q2.reshape(batch, seq, SB_WIDTH), kt3, v2.reshape(batch, seq, SB_WIDTH),
                        _tri_const(), sub=ATTN_SUB, npair=ATTN_NPAIR)
    h1 = _mixer_out(x2, ya3.reshape(tokens, SB_WIDTH), w_in, row(b_gate), row(ln_sg_g),
                    row(ln_sg_b), w_spatial, bsp, w_branch_a, w_branch_b, w_out,
                    row(b_out), row(ln1_g), row(ln1_b), tm=MIXER_TM)
    h2 = _conv_ffn(h1, w_up, conv_w, row(conv_b), w_down, row(b_down), row(ln2_g), row(ln2_b), seq,
                   tm=FFN_TM)
    return h2.reshape(batch, seq, D_MODEL)


def kernel(x, w_in, b_gate, ln_sg_g, ln_sg_b, w_spatial, b_spatial, w_branch_a, w_branch_b, w_out,
           b_out, ln1_g, ln1_b, w_up, conv_w, conv_b, w_down, b_down, ln2_g, ln2_b):
    h = x
    for l in range(w_in.shape[0]):
        h = _layer(h, w_in[l], b_gate[l], ln_sg_g[l], ln_sg_b[l], w_spatial[l], b_spatial[l],
                   w_branch_a[l], w_branch_b[l], w_out[l], b_out[l], ln1_g[l], ln1_b[l], w_up[l],
                   conv_w[l], conv_b[l], w_down[l], b_down[l], ln2_g[l], ln2_b[l])
    return h
```

```python
import functools

import jax
import jax.numpy as jnp
from jax import lax
from jax.experimental import pallas as pl
from jax.experimental.pallas import tpu as pltpu

F32 = jnp.float32
BF16 = jnp.bfloat16

D_MODEL = 1024
N_HEADS = 8
HEAD_DIM = 64
SB_WIDTH = N_HEADS * HEAD_DIM
SG_WIDTH = 512
COL_U = 3 * SB_WIDTH
COL_VSG = COL_U + SG_WIDTH
COL_GATE = COL_VSG + SG_WIDTH
N_SG_GROUPS = 8
CHUNK = 128
Q_BLOCK = 128
D_FF = 2816
LN_EPS = 1e-5
LOG2E = 1.4426950408889634
SCORE_SCALE = HEAD_DIM ** -0.5 * LOG2E
NEG_FILL = -1e30
STICK_CUTOFF = 64.0 * LOG2E
DN_ALPHA = 2.0 ** 0.25

LANES = 128
SUBLANES = 8
PAIR = 2 * HEAD_DIM
N_PAIRS = N_HEADS // 2
FF_PAIR = 256
FF_STEPS = D_FF // FF_PAIR
FF_SLABS = 2 * FF_PAIR // LANES

QKV_TM = 2048
ATTN_SUB = 8
ATTN_NPAIR = 2
MIXER_TM = 1024
FFN_TM = 512
VMEM_LIMIT = 56 * 1024 * 1024


def _ln(x, g, b):
    mu = jnp.mean(x, axis=-1, keepdims=True)
    xc = x - mu
    var = jnp.mean(xc * xc, axis=-1, keepdims=True)
    return xc * lax.rsqrt(var + LN_EPS) * g + b


def _gelu_tanh(x):
    c = 0.7978845608028654
    return x * (0.5 * (1.0 + jnp.tanh(c * (x + 0.044715 * (x * x * x)))))


def _sigmoid(x):
    return 1.0 / (1.0 + jnp.exp(-x))


def _dot(a, b):
    return lax.dot_general(a, b, (((1,), (0,)), ((), ())), preferred_element_type=F32)


def _resident(shape):
    nd = len(shape)
    return pl.BlockSpec(shape, lambda *_: (0,) * nd, pipeline_mode=pl.Buffered(1))


def _qkv_kernel(x_ref, wq_ref, wkt_ref, wv_ref, q_ref, kt_ref, v_ref):
    x = x_ref[...].astype(BF16)
    q_ref[...] = (_dot(x, wq_ref[...]) * SCORE_SCALE).astype(BF16)
    v_ref[...] = _dot(x, wv_ref[...]).astype(BF16)
    kt = lax.dot_general(wkt_ref[...], x, (((1,), (1,)), ((), ())),
                         preferred_element_type=F32)
    kt_ref[0] = kt.astype(BF16)


def _qkv_proj(x2, w_in, wkt, batch, seq, tm):
    tokens = x2.shape[0]
    tiles_per_seq = seq // tm
    col_block = lambda k: pl.BlockSpec((D_MODEL, SB_WIDTH), lambda i: (0, k), pipeline_mode=pl.Buffered(1))
    return pl.pallas_call(
        _qkv_kernel,
        grid=(tokens // tm,),
        in_specs=[
            pl.BlockSpec((tm, D_MODEL), lambda i: (i, 0)),
            col_block(0),
            _resident((SB_WIDTH, D_MODEL)),
            col_block(2),
        ],
        out_specs=[
            pl.BlockSpec((tm, SB_WIDTH), lambda i: (i, 0)),
            pl.BlockSpec((1, SB_WIDTH, tm), lambda i: (i // tiles_per_seq, 0, i % tiles_per_seq)),
            pl.BlockSpec((tm, SB_WIDTH), lambda i: (i, 0)),
        ],
        out_shape=[
            jax.ShapeDtypeStruct((tokens, SB_WIDTH), BF16),
            jax.ShapeDtypeStruct((batch, SB_WIDTH, seq), BF16),
            jax.ShapeDtypeStruct((tokens, SB_WIDTH), BF16),
        ],
        compiler_params=pltpu.CompilerParams(
            dimension_semantics=("arbitrary",), vmem_limit_bytes=VMEM_LIMIT),
        name="qkv_proj",
    )(x2, w_in, wkt, w_in)


def _softplus2(z):
    return jnp.maximum(z, 0.0) + jnp.log(1.0 + jnp.exp2(-jnp.abs(z))) * LOG2E


def _sb_attn_kernel(q_ref, kt_ref, v_ref, tri_ref, o_ref, kbd_ref, vbd_ref, pre_ref, rs_ref,
                    carry_ref, acc_ref, *, n_kblocks, sub, npair):
    qs = pl.program_id(2)
    mq = sub * Q_BLOCK
    pairs = range(npair)

    @pl.when(qs == 0)
    def _():
        row_lo = lax.broadcasted_iota(jnp.int32, (PAIR, Q_BLOCK), 0) < HEAD_DIM
        lane_lo = lax.broadcasted_iota(jnp.int32, (Q_BLOCK, PAIR), 1) < HEAD_DIM
        zero = jnp.zeros((PAIR, Q_BLOCK), BF16)
        for p in pairs:
            for j in range(n_kblocks):
                kt = kt_ref[0, p * PAIR:(p + 1) * PAIR, j * Q_BLOCK:(j + 1) * Q_BLOCK]
                kbd_ref[p, j, :, 0:Q_BLOCK] = jnp.where(row_lo, kt, zero)
                kbd_ref[p, j, :, Q_BLOCK:2 * Q_BLOCK] = jnp.where(row_lo, zero, kt)
                vt = v_ref[0, j * Q_BLOCK:(j + 1) * Q_BLOCK, p * PAIR:(p + 1) * PAIR]
                vbd_ref[p, j, 0:Q_BLOCK, :] = jnp.where(lane_lo, vt, zero)
                vbd_ref[p, j, Q_BLOCK:2 * Q_BLOCK, :] = jnp.where(lane_lo, zero, vt)
            vbd_ref[p, n_kblocks] = jnp.zeros((2 * Q_BLOCK, PAIR), BF16)

    tri = tri_ref[...]
    row = lax.broadcasted_iota(jnp.int32, (Q_BLOCK, 2 * Q_BLOCK), 0)
    col = lax.broadcasted_iota(jnp.int32, (Q_BLOCK, 2 * Q_BLOCK), 1) & (Q_BLOCK - 1)
    causal = col < row

    def mask_top(x, fill, diag):
        if not diag:
            return x
        top = jnp.where(causal, x[:Q_BLOCK], fill)
        return top if x.shape[0] == Q_BLOCK else jnp.concatenate([top, x[Q_BLOCK:]], axis=0)

    def stage1(p, r0, r1, j, diag):
        q2 = q_ref[0, r0:r1, p * PAIR:(p + 1) * PAIR]
        z = _dot(q2, kbd_ref[p, j])
        sp = mask_top(_softplus2(z), 0.0, diag)
        hi = sp.astype(BF16)
        lo = (sp - hi.astype(F32)).astype(BF16)
        cr = [_dot(jnp.concatenate([hi[:, h * Q_BLOCK:(h + 1) * Q_BLOCK],
                                    lo[:, h * Q_BLOCK:(h + 1) * Q_BLOCK]], axis=1), tri)
              for h in range(2)]
        pre = z + jnp.concatenate([cr[0][:, :Q_BLOCK], cr[1][:, :Q_BLOCK]], axis=1)
        rs = jnp.concatenate([cr[0][:, Q_BLOCK:], cr[1][:, Q_BLOCK:]], axis=1)
        return pre, rs

    def stage2(p, pre, rs, r0, r1, j, diag, jv=None):
        c = carry_ref[p, r0:r1, :]
        c_new = c + rs
        carry_ref[p, r0:r1, :] = c_new
        w = jnp.exp2(mask_top(pre + c, NEG_FILL, diag)).astype(BF16)
        acc_ref[p, r0:r1, :] += _dot(w, vbd_ref[p, j if jv is None else jv])
        return jnp.max(c_new[-Q_BLOCK:])

    def block(p, r0, r1, j, diag, jv=None):
        pre, rs = stage1(p, r0, r1, j, diag)
        return stage2(p, pre, rs, r0, r1, j, diag, jv)

    def stick_left():
        return functools.reduce(jnp.maximum, [jnp.max(carry_ref[p]) for p in pairs])

    carry_ref[...] = jnp.zeros_like(carry_ref)
    acc_ref[...] = jnp.zeros_like(acc_ref)
    j0 = qs * sub
    j1 = jnp.maximum(j0 - 1, 0)
    jv = jnp.where(qs > 0, j1, n_kblocks)

    stick = []
    for d in reversed(range(-1, sub)):
        r0, r1 = max(d, 0) * Q_BLOCK, min(d + 2, sub) * Q_BLOCK
        for p in pairs:
            left = block(p, r0, r1, j0 + d, True) if d >= 0 else block(p, r0, r1, j1, False, jv)
            if d < sub - 1:
                stick.append(left)

    @pl.when(functools.reduce(jnp.maximum, stick) >= -STICK_CUTOFF)
    def _():
        for d in reversed(range(0, sub - 2)):
            for p in pairs:
                block(p, (d + 2) * Q_BLOCK, mq, j0 + d, False)

        @pl.when(qs > 0)
        def _():
            for p in pairs:
                block(p, Q_BLOCK, mq, j0 - 1, False)

            @pl.when(jnp.logical_and(j0 >= 2, stick_left() >= -STICK_CUTOFF))
            def _():
                for p in pairs:
                    pre, rs = stage1(p, 0, mq, j0 - 2, False)
                    pre_ref[p] = pre
                    rs_ref[p] = rs

                def cond(state):
                    j, alive = state
                    return jnp.logical_and(j >= 1, alive > 0)

                def body(state):
                    j, _ = state
                    for p in pairs:
                        stage2(p, pre_ref[p], rs_ref[p], 0, mq, j, False)
                        pre_n, rs_n = stage1(p, 0, mq, j - 1, False)
                        pre_ref[p] = pre_n
                        rs_ref[p] = rs_n
                    return j - 1, (stick_left() >= -STICK_CUTOFF).astype(jnp.int32)

                j_end, alive = lax.while_loop(cond, body, (j0 - 2, jnp.int32(1)))

                @pl.when(jnp.logical_and(j_end == 0, alive > 0))
                def _():
                    for p in pairs:
                        stage2(p, pre_ref[p], rs_ref[p], 0, mq, 0, False)

    for p in pairs:
        o_ref[0, :, p * PAIR:(p + 1) * PAIR] = acc_ref[p].astype(BF16)


def _sb_attention(q3, kt3, v3, tri, sub, npair):
    batch, seq, _ = q3.shape
    n_blocks = seq // Q_BLOCK
    mq = sub * Q_BLOCK
    width = npair * PAIR
    return pl.pallas_call(
        functools.partial(_sb_attn_kernel, n_kblocks=n_blocks, sub=sub, npair=npair),
        grid=(batch, N_PAIRS // npair, seq // mq),
        in_specs=[
            pl.BlockSpec((1, mq, width), lambda b, g, i: (b, i, g)),
            pl.BlockSpec((1, width, seq), lambda b, g, i: (b, g, 0)),
            pl.BlockSpec((1, seq, width), lambda b, g, i: (b, 0, g)),
            _resident((2 * Q_BLOCK, 2 * Q_BLOCK)),
        ],
        out_specs=pl.BlockSpec((1, mq, width), lambda b, g, i: (b, i, g)),
        out_shape=jax.ShapeDtypeStruct((batch, seq, SB_WIDTH), BF16),
        scratch_shapes=[
            pltpu.VMEM((npair, n_blocks, PAIR, 2 * Q_BLOCK), BF16),
            pltpu.VMEM((npair, n_blocks + 1, 2 * Q_BLOCK, PAIR), BF16),
            pltpu.VMEM((npair, mq, 2 * Q_BLOCK), F32),
            pltpu.VMEM((npair, mq, 2 * Q_BLOCK), F32),
            pltpu.VMEM((npair, mq, 2 * Q_BLOCK), F32),
            pltpu.VMEM((npair, mq, PAIR), F32),
        ],
        compiler_params=pltpu.CompilerParams(
            dimension_semantics=("arbitrary", "arbitrary", "arbitrary"),
            vmem_limit_bytes=VMEM_LIMIT),
        name="sb_attn",
    )(q3, kt3, v3, tri)


def _mixer_out_kernel(x_ref, ya_ref, wu_ref, wvs_ref, wg0_ref, wg1_ref, wg2_ref, wg3_ref, bg_ref, lsg_ref,
                      lsb_ref, wsp_ref, bsp_ref, wa_ref, wb_ref, wo_ref, bo_ref, g1_ref, b1_ref,
                      wup_ref, wdn_ref, o_ref, wup_bf_ref, wdn_bf_ref):
    wup_bf_ref[...] = wup_ref[...].astype(BF16)
    wdn_bf_ref[...] = wdn_ref[...].astype(BF16)
    x = x_ref[...]
    xb = x.astype(BF16)
    tm = x.shape[0]

    u = _gelu_tanh(_dot(xb, wu_ref[...]))
    vs = _ln(_gelu_tanh(_dot(xb, wvs_ref[...])), lsg_ref[...], lsb_ref[...]).astype(BF16)

    tril = (lax.broadcasted_iota(jnp.int32, (CHUNK, CHUNK), 1)
            <= lax.broadcasted_iota(jnp.int32, (CHUNK, CHUNK), 0))
    wsp = [jnp.where(tril, wsp_ref[g], 0.0).astype(BF16) for g in range(N_SG_GROUPS)]
    lane_lo = lax.broadcasted_iota(jnp.int32, (CHUNK, LANES), 1) < HEAD_DIM
    zero = jnp.zeros((CHUNK, LANES), BF16)
    rows = []
    for c in range(tm // CHUNK):
        parts = []
        for p in range(N_SG_GROUPS // 2):
            v2 = vs[c * CHUNK:(c + 1) * CHUNK, p * LANES:(p + 1) * LANES]
            parts.append(_dot(wsp[2 * p], jnp.where(lane_lo, v2, zero))
                         + _dot(wsp[2 * p + 1], jnp.where(lane_lo, zero, v2)))
        rows.append(jnp.concatenate(parts, axis=1) + bsp_ref[...])
    yb = (u * jnp.concatenate(rows, axis=0)).astype(BF16)

    ga = _sigmoid(jnp.concatenate([_dot(xb, wg0_ref[...]), _dot(xb, wg1_ref[...])], axis=1) + bg_ref[:, :D_MODEL])
    gb = _sigmoid(jnp.concatenate([_dot(xb, wg2_ref[...]), _dot(xb, wg3_ref[...])], axis=1) + bg_ref[:, D_MODEL:])
    merged = ga * _dot(ya_ref[...], wa_ref[...]) + gb * _dot(yb, wb_ref[...])
    y = _dot(merged.astype(BF16), wo_ref[...]) + bo_ref[...]
    o_ref[...] = _ln(DN_ALPHA * x + y, g1_ref[...], b1_ref[...])


def _mixer_out(x2, ya2, w_in, bg, lsg, lsb, wsp, bsp, wa, wb, wo, bo, g1, b1, w_up, w_down, tm):
    tokens = x2.shape[0]
    steps = tokens // tm
    col_block = lambda k: pl.BlockSpec((D_MODEL, SB_WIDTH), lambda i: (0, k), pipeline_mode=pl.Buffered(1))
    up_rows = pl.BlockSpec((D_MODEL // steps, 2 * D_FF), lambda i: (i, 0))
    dn_rows = pl.BlockSpec((D_FF // steps, D_MODEL), lambda i: (i, 0))
    return pl.pallas_call(
        _mixer_out_kernel,
        grid=(tokens // tm,),
        in_specs=[
            pl.BlockSpec((tm, D_MODEL), lambda i: (i, 0)),
            pl.BlockSpec((tm, SB_WIDTH), lambda i: (i, 0)),
            col_block(COL_U // SB_WIDTH), col_block(COL_VSG // SB_WIDTH),
            *[col_block(COL_GATE // SB_WIDTH + k) for k in range(2 * D_MODEL // SB_WIDTH)],
            _resident(bg.shape),
            _resident(lsg.shape), _resident(lsb.shape), _resident(wsp.shape), _resident(bsp.shape),
            _resident(wa.shape), _resident(wb.shape), _resident(wo.shape), _resident(bo.shape),
            _resident(g1.shape), _resident(b1.shape),
            up_rows, dn_rows,
        ],
        out_specs=[pl.BlockSpec((tm, D_MODEL), lambda i: (i, 0)), up_rows, dn_rows],
        out_shape=[jax.ShapeDtypeStruct((tokens, D_MODEL), F32),
                   jax.ShapeDtypeStruct(w_up.shape, BF16), jax.ShapeDtypeStruct(w_down.shape, BF16)],
        compiler_params=pltpu.CompilerParams(
            dimension_semantics=("arbitrary",), vmem_limit_bytes=VMEM_LIMIT),
        name="mixer_out",
    )(x2, ya2, w_in, w_in, w_in, w_in, w_in, w_in, bg, lsg, lsb, wsp, bsp, wa, wb, wo, bo, g1, b1,
      w_up, w_down)


def _conv_ffn_kernel(h_ref, wup_ref, cw_ref, cb_ref, wd_ref, bd_ref, g2_ref, b2_ref, o_ref,
                     slab_ref, tail_ref, acc_ref, *, tiles_per_seq):
    tm = h_ref.shape[0]
    half = FF_SLABS // 2
    PRE = SUBLANES
    seq_start = (pl.program_id(0) % tiles_per_seq) == 0
    hb = h_ref[...].astype(BF16)

    def produce(c, slot):
        for part in range(2):
            c0 = part * D_FF + c * FF_PAIR
            up = _dot(hb, wup_ref[:, c0:c0 + FF_PAIR])
            for s in range(half):
                slab_ref[slot, part * half + s, PRE:tm + PRE, :] = up[:, s * LANES:(s + 1) * LANES]

    def consume(c, slot):
        conv = []
        for s in range(FF_SLABS):
            c0 = (s // half) * D_FF + c * FF_PAIR + (s % half) * LANES
            cw = cw_ref[:, c0:c0 + LANES]
            slab_ref[slot, s, 0:PRE, :] = jnp.where(seq_start, 0.0, tail_ref[c, s])
            tail_ref[c, s] = slab_ref[slot, s, tm:tm + PRE, :]
            conv.append(cb_ref[:, c0:c0 + LANES]
                        + cw[0:1] * slab_ref[slot, s, PRE - 2:tm + PRE - 2, :]
                        + cw[1:2] * slab_ref[slot, s, PRE - 1:tm + PRE - 1, :]
                        + cw[2:3] * slab_ref[slot, s, PRE:tm + PRE, :])
        act = jnp.concatenate([(conv[s] * _sigmoid(conv[s]) * conv[half + s]).astype(BF16)
                               for s in range(half)], axis=1)
        acc_ref[...] += _dot(act, wd_ref[c * FF_PAIR:(c + 1) * FF_PAIR, :])

    acc_ref[...] = DN_ALPHA * h_ref[...] + bd_ref[...]
    produce(0, 0)
    for c in range(FF_STEPS):
        if c + 1 < FF_STEPS:
            produce(c + 1, (c + 1) % 2)
        consume(c, c % 2)
    o_ref[...] = _ln(acc_ref[...], g2_ref[...], b2_ref[...])


def _conv_ffn(h2, wup, cw, cb, wd, bd, g2, b2, seq, tm):
    tokens = h2.shape[0]
    return pl.pallas_call(
        functools.partial(_conv_ffn_kernel, tiles_per_seq=seq // tm),
        grid=(tokens // tm,),
        in_specs=[
            pl.BlockSpec((tm, D_MODEL), lambda i: (i, 0)),
            _resident(wup.shape), _resident(cw.shape), _resident(cb.shape), _resident(wd.shape),
            _resident(bd.shape), _resident(g2.shape), _resident(b2.shape),
        ],
        out_specs=pl.BlockSpec((tm, D_MODEL), lambda i: (i, 0)),
        out_shape=jax.ShapeDtypeStruct((tokens, D_MODEL), F32),
        scratch_shapes=[
            pltpu.VMEM((2, FF_SLABS, tm + SUBLANES, LANES), F32),
            pltpu.VMEM((FF_STEPS, FF_SLABS, SUBLANES, LANES), F32),
            pltpu.VMEM((tm, D_MODEL), F32),
        ],
        compiler_params=pltpu.CompilerParams(
            dimension_semantics=("arbitrary",), vmem_limit_bytes=VMEM_LIMIT),
        name="conv_ffn",
    )(h2, wup, cw, cb, wd, bd, g2, b2)


def _tri_const():
    j = jnp.arange(Q_BLOCK)[:, None]
    s = jnp.arange(Q_BLOCK)[None, :]
    half = jnp.concatenate([-(j >= s).astype(F32), -jnp.ones((Q_BLOCK, Q_BLOCK), F32)], axis=1)
    return jnp.concatenate([half, half], axis=0).astype(BF16)


def _layer(h, w_in, b_gate, ln_sg_g, ln_sg_b, w_spatial, b_spatial, w_branch_a, w_branch_b, w_out,
           b_out, ln1_g, ln1_b, w_up, conv_w, conv_b, w_down, b_down, ln2_g, ln2_b):
    batch, seq, _ = h.shape
    tokens = batch * seq
    x2 = h.reshape(tokens, D_MODEL)
    row = lambda v: v.reshape(1, -1)

    wkt = w_in[:, SB_WIDTH:2 * SB_WIDTH].T
    bsp = jnp.repeat(b_spatial.T, SG_WIDTH // N_SG_GROUPS, axis=1)

    q2, kt3, v2 = _qkv_proj(x2, w_in, wkt, batch, seq, tm=QKV_TM)
    ya3 = _sb_attention(q2.reshape(batch, seq, SB_WIDTH), kt3, v2.reshape(batch, seq, SB_WIDTH),
                        _tri_const(), sub=ATTN_SUB, npair=ATTN_NPAIR)
    h1, w_up_bf, w_down_bf = _mixer_out(x2, ya3.reshape(tokens, SB_WIDTH), w_in, row(b_gate), row(ln_sg_g),
                                        row(ln_sg_b), w_spatial, bsp, w_branch_a, w_branch_b, w_out,
                                        row(b_out), row(ln1_g), row(ln1_b), w_up, w_down, tm=MIXER_TM)
    h2 = _conv_ffn(h1, w_up_bf, conv_w, row(conv_b), w_down_bf, row(b_down), row(ln2_g), row(ln2_b), seq,
                   tm=FFN_TM)
    return h2.reshape(batch, seq, D_MODEL)


def kernel(x, w_in, b_gate, ln_sg_g, ln_sg_b, w_spatial, b_spatial, w_branch_a, w_branch_b, w_out,
           b_out, ln1_g, ln1_b, w_up, conv_w, conv_b, w_down, b_down, ln2_g, ln2_b):
    h = x
    for l in range(w_in.shape[0]):
        h = _layer(h, w_in[l], b_gate[l], ln_sg_g[l], ln_sg_b[l], w_spatial[l], b_spatial[l],
                   w_branch_a[l], w_branch_b[l], w_out[l], b_out[l], ln1_g[l], ln1_b[l], w_up[l],
                   conv_w[l], conv_b[l], w_down[l], b_down[l], ln2_g[l], ln2_b[l])
    return h
```

```python
import functools

import jax
import jax.numpy as jnp
from jax import lax
from jax.experimental import pallas as pl
from jax.experimental.pallas import tpu as pltpu

F32 = jnp.float32
BF16 = jnp.bfloat16

D_MODEL = 1024
N_HEADS = 8
HEAD_DIM = 64
SB_WIDTH = N_HEADS * HEAD_DIM
SG_WIDTH = 512
COL_U = 3 * SB_WIDTH
COL_VSG = COL_U + SG_WIDTH
COL_GATE = COL_VSG + SG_WIDTH
N_SG_GROUPS = 8
CHUNK = 128
Q_BLOCK = 128
D_FF = 2816
LN_EPS = 1e-5
LOG2E = 1.4426950408889634
SCORE_SCALE = HEAD_DIM ** -0.5 * LOG2E
NEG_FILL = -1e30
STICK_CUTOFF = 64.0 * LOG2E
DN_ALPHA = 2.0 ** 0.25

LANES = 128
SUBLANES = 8
PAIR = 2 * HEAD_DIM
N_PAIRS = N_HEADS // 2
FF_PAIR = 256
FF_STEPS = D_FF // FF_PAIR
FF_SLABS = 2 * FF_PAIR // LANES

QKV_TM = 2048
ATTN_SUB = 8
ATTN_NPAIR = 2
MIXER_TM = 1024
FFN_TM = 512
VMEM_LIMIT = 56 * 1024 * 1024


def _ln(x, g, b):
    mu = jnp.mean(x, axis=-1, keepdims=True)
    xc = x - mu
    var = jnp.mean(xc * xc, axis=-1, keepdims=True)
    return xc * lax.rsqrt(var + LN_EPS) * g + b


def _gelu_tanh(x):
    c = 0.7978845608028654
    return x * (0.5 * (1.0 + jnp.tanh(c * (x + 0.044715 * (x * x * x)))))


def _sigmoid(x):
    return 1.0 / (1.0 + jnp.exp(-x))


def _dot(a, b):
    return lax.dot_general(a, b, (((1,), (0,)), ((), ())), preferred_element_type=F32)


def _resident(shape):
    nd = len(shape)
    return pl.BlockSpec(shape, lambda *_: (0,) * nd, pipeline_mode=pl.Buffered(1))


def _qkv_kernel(x_ref, wq_ref, wkt_ref, wv_ref, q_ref, kt_ref, v_ref):
    x = x_ref[...].astype(BF16)
    q_ref[...] = (_dot(x, wq_ref[...]) * SCORE_SCALE).astype(BF16)
    v_ref[...] = _dot(x, wv_ref[...]).astype(BF16)
    kt = lax.dot_general(wkt_ref[...], x, (((1,), (1,)), ((), ())),
                         preferred_element_type=F32)
    kt_ref[0] = kt.astype(BF16)


def _qkv_proj(x2, w_in, wkt, batch, seq, tm):
    tokens = x2.shape[0]
    tiles_per_seq = seq // tm
    col_block = lambda k: pl.BlockSpec((D_MODEL, SB_WIDTH), lambda i: (0, k), pipeline_mode=pl.Buffered(1))
    return pl.pallas_call(
        _qkv_kernel,
        grid=(tokens // tm,),
        in_specs=[
            pl.BlockSpec((tm, D_MODEL), lambda i: (i, 0)),
            col_block(0),
            _resident((SB_WIDTH, D_MODEL)),
            col_block(2),
        ],
        out_specs=[
            pl.BlockSpec((tm, SB_WIDTH), lambda i: (i, 0)),
            pl.BlockSpec((1, SB_WIDTH, tm), lambda i: (i // tiles_per_seq, 0, i % tiles_per_seq)),
            pl.BlockSpec((tm, SB_WIDTH), lambda i: (i, 0)),
        ],
        out_shape=[
            jax.ShapeDtypeStruct((tokens, SB_WIDTH), BF16),
            jax.ShapeDtypeStruct((batch, SB_WIDTH, seq), BF16),
            jax.ShapeDtypeStruct((tokens, SB_WIDTH), BF16),
        ],
        compiler_params=pltpu.CompilerParams(
            dimension_semantics=("arbitrary",), vmem_limit_bytes=VMEM_LIMIT),
        name="qkv_proj",
    )(x2, w_in, wkt, w_in)


def _softplus2(z):
    return jnp.maximum(z, 0.0) + jnp.log(1.0 + jnp.exp2(-jnp.abs(z))) * LOG2E


def _sb_attn_kernel(q_ref, kt_ref, v_ref, tri_ref, *refs, n_kblocks, sub, npair, n_cast):
    w_refs, (o_ref, *w_bf_refs) = refs[:n_cast], refs[n_cast:2 * n_cast + 1]
    kbd_ref, vbd_ref, pre_ref, rs_ref, carry_ref, acc_ref = refs[2 * n_cast + 1:]
    for w_ref, w_bf_ref in zip(w_refs, w_bf_refs):
        w_bf_ref[...] = w_ref[...].astype(BF16)
    _sb_attn_body(q_ref, kt_ref, v_ref, tri_ref, o_ref, kbd_ref, vbd_ref, pre_ref, rs_ref, carry_ref, acc_ref,
                  n_kblocks=n_kblocks, sub=sub, npair=npair)


def _sb_attn_body(q_ref, kt_ref, v_ref, tri_ref, o_ref, kbd_ref, vbd_ref, pre_ref, rs_ref,
                  carry_ref, acc_ref, *, n_kblocks, sub, npair):
    qs = pl.program_id(2)
    mq = sub * Q_BLOCK
    pairs = range(npair)

    @pl.when(qs == 0)
    def _():
        row_lo = lax.broadcasted_iota(jnp.int32, (PAIR, Q_BLOCK), 0) < HEAD_DIM
        lane_lo = lax.broadcasted_iota(jnp.int32, (Q_BLOCK, PAIR), 1) < HEAD_DIM
        zero = jnp.zeros((PAIR, Q_BLOCK), BF16)
        for p in pairs:
            for j in range(n_kblocks):
                kt = kt_ref[0, p * PAIR:(p + 1) * PAIR, j * Q_BLOCK:(j + 1) * Q_BLOCK]
                kbd_ref[p, j, :, 0:Q_BLOCK] = jnp.where(row_lo, kt, zero)
                kbd_ref[p, j, :, Q_BLOCK:2 * Q_BLOCK] = jnp.where(row_lo, zero, kt)
                vt = v_ref[0, j * Q_BLOCK:(j + 1) * Q_BLOCK, p * PAIR:(p + 1) * PAIR]
                vbd_ref[p, j, 0:Q_BLOCK, :] = jnp.where(lane_lo, vt, zero)
                vbd_ref[p, j, Q_BLOCK:2 * Q_BLOCK, :] = jnp.where(lane_lo, zero, vt)
            vbd_ref[p, n_kblocks] = jnp.zeros((2 * Q_BLOCK, PAIR), BF16)

    tri = tri_ref[...]
    row = lax.broadcasted_iota(jnp.int32, (Q_BLOCK, 2 * Q_BLOCK), 0)
    col = lax.broadcasted_iota(jnp.int32, (Q_BLOCK, 2 * Q_BLOCK), 1) & (Q_BLOCK - 1)
    causal = col < row

    def mask_top(x, fill, diag):
        if not diag:
            return x
        top = jnp.where(causal, x[:Q_BLOCK], fill)
        return top if x.shape[0] == Q_BLOCK else jnp.concatenate([top, x[Q_BLOCK:]], axis=0)

    def stage1(p, r0, r1, j, diag):
        q2 = q_ref[0, r0:r1, p * PAIR:(p + 1) * PAIR]
        z = _dot(q2, kbd_ref[p, j])
        sp = mask_top(_softplus2(z), 0.0, diag)
        hi = sp.astype(BF16)
        lo = (sp - hi.astype(F32)).astype(BF16)
        cr = [_dot(jnp.concatenate([hi[:, h * Q_BLOCK:(h + 1) * Q_BLOCK],
                                    lo[:, h * Q_BLOCK:(h + 1) * Q_BLOCK]], axis=1), tri)
              for h in range(2)]
        pre = z + jnp.concatenate([cr[0][:, :Q_BLOCK], cr[1][:, :Q_BLOCK]], axis=1)
        rs = jnp.concatenate([cr[0][:, Q_BLOCK:], cr[1][:, Q_BLOCK:]], axis=1)
        return pre, rs

    def stage2(p, pre, rs, r0, r1, j, diag, jv=None):
        c = carry_ref[p, r0:r1, :]
        c_new = c + rs
        carry_ref[p, r0:r1, :] = c_new
        w = jnp.exp2(mask_top(pre + c, NEG_FILL, diag)).astype(BF16)
        acc_ref[p, r0:r1, :] += _dot(w, vbd_ref[p, j if jv is None else jv])
        return jnp.max(c_new[-Q_BLOCK:])

    def block(p, r0, r1, j, diag, jv=None):
        pre, rs = stage1(p, r0, r1, j, diag)
        return stage2(p, pre, rs, r0, r1, j, diag, jv)

    def stick_left():
        return functools.reduce(jnp.maximum, [jnp.max(carry_ref[p]) for p in pairs])

    carry_ref[...] = jnp.zeros_like(carry_ref)
    acc_ref[...] = jnp.zeros_like(acc_ref)
    j0 = qs * sub
    j1 = jnp.maximum(j0 - 1, 0)
    jv = jnp.where(qs > 0, j1, n_kblocks)

    stick = []
    for d in reversed(range(-1, sub)):
        r0, r1 = max(d, 0) * Q_BLOCK, min(d + 2, sub) * Q_BLOCK
        for p in pairs:
            left = block(p, r0, r1, j0 + d, True) if d >= 0 else block(p, r0, r1, j1, False, jv)
            if d < sub - 1:
                stick.append(left)

    @pl.when(functools.reduce(jnp.maximum, stick) >= -STICK_CUTOFF)
    def _():
        for d in reversed(range(0, sub - 2)):
            for p in pairs:
                block(p, (d + 2) * Q_BLOCK, mq, j0 + d, False)

        @pl.when(qs > 0)
        def _():
            for p in pairs:
                block(p, Q_BLOCK, mq, j0 - 1, False)

            @pl.when(jnp.logical_and(j0 >= 2, stick_left() >= -STICK_CUTOFF))
            def _():
                for p in pairs:
                    pre, rs = stage1(p, 0, mq, j0 - 2, False)
                    pre_ref[p] = pre
                    rs_ref[p] = rs

                def cond(state):
                    j, alive = state
                    return jnp.logical_and(j >= 1, alive > 0)

                def body(state):
                    j, _ = state
                    for p in pairs:
                        stage2(p, pre_ref[p], rs_ref[p], 0, mq, j, False)
                        pre_n, rs_n = stage1(p, 0, mq, j - 1, False)
                        pre_ref[p] = pre_n
                        rs_ref[p] = rs_n
                    return j - 1, (stick_left() >= -STICK_CUTOFF).astype(jnp.int32)

                j_end, alive = lax.while_loop(cond, body, (j0 - 2, jnp.int32(1)))

                @pl.when(jnp.logical_and(j_end == 0, alive > 0))
                def _():
                    for p in pairs:
                        stage2(p, pre_ref[p], rs_ref[p], 0, mq, 0, False)

    for p in pairs:
        o_ref[0, :, p * PAIR:(p + 1) * PAIR] = acc_ref[p].astype(BF16)


def _sb_attention(q3, kt3, v3, tri, weights, sub, npair):
    batch, seq, _ = q3.shape
    n_blocks = seq // Q_BLOCK
    mq = sub * Q_BLOCK
    width = npair * PAIR
    groups, qsteps = N_PAIRS // npair, seq // mq
    steps = batch * groups * qsteps
    w_rows = [pl.BlockSpec((w.shape[0] // steps, w.shape[1]), lambda b, g, i: ((b * groups + g) * qsteps + i, 0))
              for w in weights]
    return pl.pallas_call(
        functools.partial(_sb_attn_kernel, n_kblocks=n_blocks, sub=sub, npair=npair, n_cast=len(weights)),
        grid=(batch, groups, qsteps),
        in_specs=[
            pl.BlockSpec((1, mq, width), lambda b, g, i: (b, i, g)),
            pl.BlockSpec((1, width, seq), lambda b, g, i: (b, g, 0)),
            pl.BlockSpec((1, seq, width), lambda b, g, i: (b, 0, g)),
            _resident((2 * Q_BLOCK, 2 * Q_BLOCK)),
            *w_rows,
        ],
        out_specs=[pl.BlockSpec((1, mq, width), lambda b, g, i: (b, i, g)), *w_rows],
        out_shape=[jax.ShapeDtypeStruct((batch, seq, SB_WIDTH), BF16),
                   *[jax.ShapeDtypeStruct(w.shape, BF16) for w in weights]],
        scratch_shapes=[
            pltpu.VMEM((npair, n_blocks, PAIR, 2 * Q_BLOCK), BF16),
            pltpu.VMEM((npair, n_blocks + 1, 2 * Q_BLOCK, PAIR), BF16),
            pltpu.VMEM((npair, mq, 2 * Q_BLOCK), F32),
            pltpu.VMEM((npair, mq, 2 * Q_BLOCK), F32),
            pltpu.VMEM((npair, mq, 2 * Q_BLOCK), F32),
            pltpu.VMEM((npair, mq, PAIR), F32),
        ],
        compiler_params=pltpu.CompilerParams(
            dimension_semantics=("arbitrary", "arbitrary", "arbitrary"),
            vmem_limit_bytes=VMEM_LIMIT),
        name="sb_attn",
    )(q3, kt3, v3, tri, *weights)


def _mixer_out_kernel(x_ref, ya_ref, wu_ref, wvs_ref, wg0_ref, wg1_ref, wg2_ref, wg3_ref, bg_ref, lsg_ref,
                      lsb_ref, wsp_ref, bsp_ref, wa_ref, wb_ref, wo_ref, bo_ref, g1_ref, b1_ref,
                      wup_ref, wdn_ref, o_ref, wup_bf_ref, wdn_bf_ref):
    wup_bf_ref[...] = wup_ref[...].astype(BF16)
    wdn_bf_ref[...] = wdn_ref[...].astype(BF16)
    x = x_ref[...]
    xb = x.astype(BF16)
    tm = x.shape[0]

    u = _gelu_tanh(_dot(xb, wu_ref[...]))
    vs = _ln(_gelu_tanh(_dot(xb, wvs_ref[...])), lsg_ref[...], lsb_ref[...]).astype(BF16)

    tril = (lax.broadcasted_iota(jnp.int32, (CHUNK, CHUNK), 1)
            <= lax.broadcasted_iota(jnp.int32, (CHUNK, CHUNK), 0))
    wsp = [jnp.where(tril, wsp_ref[g], 0.0).astype(BF16) for g in range(N_SG_GROUPS)]
    lane_lo = lax.broadcasted_iota(jnp.int32, (CHUNK, LANES), 1) < HEAD_DIM
    zero = jnp.zeros((CHUNK, LANES), BF16)
    rows = []
    for c in range(tm // CHUNK):
        parts = []
        for p in range(N_SG_GROUPS // 2):
            v2 = vs[c * CHUNK:(c + 1) * CHUNK, p * LANES:(p + 1) * LANES]
            parts.append(_dot(wsp[2 * p], jnp.where(lane_lo, v2, zero))
                         + _dot(wsp[2 * p + 1], jnp.where(lane_lo, zero, v2)))
        rows.append(jnp.concatenate(parts, axis=1) + bsp_ref[...])
    yb = (u * jnp.concatenate(rows, axis=0)).astype(BF16)

    ga = _sigmoid(jnp.concatenate([_dot(xb, wg0_ref[...]), _dot(xb, wg1_ref[...])], axis=1) + bg_ref[:, :D_MODEL])
    gb = _sigmoid(jnp.concatenate([_dot(xb, wg2_ref[...]), _dot(xb, wg3_ref[...])], axis=1) + bg_ref[:, D_MODEL:])
    merged = ga * _dot(ya_ref[...], wa_ref[...]) + gb * _dot(yb, wb_ref[...])
    y = _dot(merged.astype(BF16), wo_ref[...]) + bo_ref[...]
    o_ref[...] = _ln(DN_ALPHA * x + y, g1_ref[...], b1_ref[...])


def _mixer_out(x2, ya2, w_in, bg, lsg, lsb, wsp, bsp, wa, wb, wo, bo, g1, b1, w_up, w_down, tm):
    tokens = x2.shape[0]
    steps = tokens // tm
    col_block = lambda k: pl.BlockSpec((D_MODEL, SB_WIDTH), lambda i: (0, k), pipeline_mode=pl.Buffered(1))
    up_rows = pl.BlockSpec((D_MODEL // steps, 2 * D_FF), lambda i: (i, 0))
    dn_rows = pl.BlockSpec((D_FF // steps, D_MODEL), lambda i: (i, 0))
    return pl.pallas_call(
        _mixer_out_kernel,
        grid=(tokens // tm,),
        in_specs=[
            pl.BlockSpec((tm, D_MODEL), lambda i: (i, 0)),
            pl.BlockSpec((tm, SB_WIDTH), lambda i: (i, 0)),
            col_block(COL_U // SB_WIDTH), col_block(COL_VSG // SB_WIDTH),
            *[col_block(COL_GATE // SB_WIDTH + k) for k in range(2 * D_MODEL // SB_WIDTH)],
            _resident(bg.shape),
            _resident(lsg.shape), _resident(lsb.shape), _resident(wsp.shape), _resident(bsp.shape),
            _resident(wa.shape), _resident(wb.shape), _resident(wo.shape), _resident(bo.shape),
            _resident(g1.shape), _resident(b1.shape),
            up_rows, dn_rows,
        ],
        out_specs=[pl.BlockSpec((tm, D_MODEL), lambda i: (i, 0)), up_rows, dn_rows],
        out_shape=[jax.ShapeDtypeStruct((tokens, D_MODEL), F32),
                   jax.ShapeDtypeStruct(w_up.shape, BF16), jax.ShapeDtypeStruct(w_down.shape, BF16)],
        compiler_params=pltpu.CompilerParams(
            dimension_semantics=("arbitrary",), vmem_limit_bytes=VMEM_LIMIT),
        name="mixer_out",
    )(x2, ya2, w_in, w_in, w_in, w_in, w_in, w_in, bg, lsg, lsb, wsp, bsp, wa, wb, wo, bo, g1, b1,
      w_up, w_down)


def _conv_ffn_kernel(h_ref, wup_ref, cw_ref, cb_ref, wd_ref, bd_ref, g2_ref, b2_ref, o_ref,
                     slab_ref, tail_ref, acc_ref, *, tiles_per_seq):
    tm = h_ref.shape[0]
    half = FF_SLABS // 2
    PRE = SUBLANES
    seq_start = (pl.program_id(0) % tiles_per_seq) == 0
    hb = h_ref[...].astype(BF16)

    def produce(c, slot):
        for part in range(2):
            c0 = part * D_FF + c * FF_PAIR
            up = _dot(hb, wup_ref[:, c0:c0 + FF_PAIR])
            for s in range(half):
                slab_ref[slot, part * half + s, PRE:tm + PRE, :] = up[:, s * LANES:(s + 1) * LANES]

    def consume(c, slot):
        conv = []
        for s in range(FF_SLABS):
            c0 = (s // half) * D_FF + c * FF_PAIR + (s % half) * LANES
            cw = cw_ref[:, c0:c0 + LANES]
            slab_ref[slot, s, 0:PRE, :] = jnp.where(seq_start, 0.0, tail_ref[c, s])
            tail_ref[c, s] = slab_ref[slot, s, tm:tm + PRE, :]
            conv.append(cb_ref[:, c0:c0 + LANES]
                        + cw[0:1] * slab_ref[slot, s, PRE - 2:tm + PRE - 2, :]
                        + cw[1:2] * slab_ref[slot, s, PRE - 1:tm + PRE - 1, :]
                        + cw[2:3] * slab_ref[slot, s, PRE:tm + PRE, :])
        act = jnp.concatenate([(conv[s] * _sigmoid(conv[s]) * conv[half + s]).astype(BF16)
                               for s in range(half)], axis=1)
        acc_ref[...] += _dot(act, wd_ref[c * FF_PAIR:(c + 1) * FF_PAIR, :])

    acc_ref[...] = DN_ALPHA * h_ref[...] + bd_ref[...]
    produce(0, 0)
    for c in range(FF_STEPS):
        if c + 1 < FF_STEPS:
            produce(c + 1, (c + 1) % 2)
        consume(c, c % 2)
    o_ref[...] = _ln(acc_ref[...], g2_ref[...], b2_ref[...])


def _conv_ffn(h2, wup, cw, cb, wd, bd, g2, b2, seq, tm):
    tokens = h2.shape[0]
    return pl.pallas_call(
        functools.partial(_conv_ffn_kernel, tiles_per_seq=seq // tm),
        grid=(tokens // tm,),
        in_specs=[
            pl.BlockSpec((tm, D_MODEL), lambda i: (i, 0)),
            _resident(wup.shape), _resident(cw.shape), _resident(cb.shape), _resident(wd.shape),
            _resident(bd.shape), _resident(g2.shape), _resident(b2.shape),
        ],
        out_specs=pl.BlockSpec((tm, D_MODEL), lambda i: (i, 0)),
        out_shape=jax.ShapeDtypeStruct((tokens, D_MODEL), F32),
        scratch_shapes=[
            pltpu.VMEM((2, FF_SLABS, tm + SUBLANES, LANES), F32),
            pltpu.VMEM((FF_STEPS, FF_SLABS, SUBLANES, LANES), F32),
            pltpu.VMEM((tm, D_MODEL), F32),
        ],
        compiler_params=pltpu.CompilerParams(
            dimension_semantics=("arbitrary",), vmem_limit_bytes=VMEM_LIMIT),
        name="conv_ffn",
    )(h2, wup, cw, cb, wd, bd, g2, b2)


def _tri_const():
    j = jnp.arange(Q_BLOCK)[:, None]
    s = jnp.arange(Q_BLOCK)[None, :]
    half = jnp.concatenate([-(j >= s).astype(F32), -jnp.ones((Q_BLOCK, Q_BLOCK), F32)], axis=1)
    return jnp.concatenate([half, half], axis=0).astype(BF16)


def _layer(h, w_in, b_gate, ln_sg_g, ln_sg_b, w_spatial, b_spatial, w_branch_a, w_branch_b, w_out,
           b_out, ln1_g, ln1_b, w_up, conv_w, conv_b, w_down, b_down, ln2_g, ln2_b):
    batch, seq, _ = h.shape
    tokens = batch * seq
    x2 = h.reshape(tokens, D_MODEL)
    row = lambda v: v.reshape(1, -1)

    wkt = w_in[:, SB_WIDTH:2 * SB_WIDTH].T
    bsp = jnp.repeat(b_spatial.T, SG_WIDTH // N_SG_GROUPS, axis=1)

    q2, kt3, v2 = _qkv_proj(x2, w_in, wkt, batch, seq, tm=QKV_TM)
    ya3, w_in_bf, wa_bf, wb_bf, wo_bf = _sb_attention(
        q2.reshape(batch, seq, SB_WIDTH), kt3, v2.reshape(batch, seq, SB_WIDTH), _tri_const(),
        (w_in, w_branch_a, w_branch_b, w_out), sub=ATTN_SUB, npair=ATTN_NPAIR)
    h1, w_up_bf, w_down_bf = _mixer_out(x2, ya3.reshape(tokens, SB_WIDTH), w_in_bf, row(b_gate), row(ln_sg_g),
                                        row(ln_sg_b), w_spatial, bsp, wa_bf, wb_bf, wo_bf,
                                        row(b_out), row(ln1_g), row(ln1_b), w_up, w_down, tm=MIXER_TM)
    h2 = _conv_ffn(h1, w_up_bf, conv_w, row(conv_b), w_down_bf, row(b_down), row(ln2_g), row(ln2_b), seq,
                   tm=FFN_TM)
    return h2.reshape(batch, seq, D_MODEL)


def kernel(x, w_in, b_gate, ln_sg_g, ln_sg_b, w_spatial, b_spatial, w_branch_a, w_branch_b, w_out,
           b_out, ln1_g, ln1_b, w_up, conv_w, conv_b, w_down, b_down, ln2_g, ln2_b):
    h = x
    for l in range(w_in.shape[0]):
        h = _layer(h, w_in[l], b_gate[l], ln_sg_g[l], ln_sg_b[l], w_spatial[l], b_spatial[l],
                   w_branch_a[l], w_branch_b[l], w_out[l], b_out[l], ln1_g[l], ln1_b[l], w_up[l],
                   conv_w[l], conv_b[l], w_down[l], b_down[l], ln2_g[l], ln2_b[l])
    return h
```

```python
import functools

import jax
import jax.numpy as jnp
from jax import lax
from jax.experimental import pallas as pl
from jax.experimental.pallas import tpu as pltpu

F32 = jnp.float32
BF16 = jnp.bfloat16

D_MODEL = 1024
N_HEADS = 8
HEAD_DIM = 64
SB_WIDTH = N_HEADS * HEAD_DIM
SG_WIDTH = 512
COL_U = 3 * SB_WIDTH
COL_VSG = COL_U + SG_WIDTH
COL_GATE = COL_VSG + SG_WIDTH
N_SG_GROUPS = 8
CHUNK = 128
Q_BLOCK = 128
D_FF = 2816
LN_EPS = 1e-5
LOG2E = 1.4426950408889634
SCORE_SCALE = HEAD_DIM ** -0.5 * LOG2E
NEG_FILL = -1e30
STICK_CUTOFF = 64.0 * LOG2E
DN_ALPHA = 2.0 ** 0.25

LANES = 128
SUBLANES = 8
PAIR = 2 * HEAD_DIM
N_PAIRS = N_HEADS // 2
FF_PAIR = 256
FF_STEPS = D_FF // FF_PAIR
FF_SLABS = 2 * FF_PAIR // LANES

QKV_TM = 2048
ATTN_SUB = 8
ATTN_NPAIR = 2
MIXER_TM = 1024
FFN_TM = 1024
FF_ROW_SPLIT = 4
VMEM_LIMIT = 56 * 1024 * 1024


def _ln(x, g, b):
    mu = jnp.mean(x, axis=-1, keepdims=True)
    xc = x - mu
    var = jnp.mean(xc * xc, axis=-1, keepdims=True)
    return xc * lax.rsqrt(var + LN_EPS) * g + b


def _gelu_tanh(x):
    c = 0.7978845608028654
    return x * (0.5 * (1.0 + jnp.tanh(c * (x + 0.044715 * (x * x * x)))))


def _sigmoid(x):
    return 1.0 / (1.0 + jnp.exp(-x))


def _dot(a, b):
    return lax.dot_general(a, b, (((1,), (0,)), ((), ())), preferred_element_type=F32)


def _resident(shape):
    nd = len(shape)
    return pl.BlockSpec(shape, lambda *_: (0,) * nd, pipeline_mode=pl.Buffered(1))


def _qkv_kernel(x_ref, wq_ref, wkt_ref, wv_ref, q_ref, kt_ref, v_ref):
    x = x_ref[...].astype(BF16)
    q_ref[...] = (_dot(x, wq_ref[...]) * SCORE_SCALE).astype(BF16)
    v_ref[...] = _dot(x, wv_ref[...]).astype(BF16)
    kt = lax.dot_general(wkt_ref[...], x, (((1,), (1,)), ((), ())),
                         preferred_element_type=F32)
    kt_ref[0] = kt.astype(BF16)


def _qkv_proj(x2, w_in, wkt, batch, seq, tm):
    tokens = x2.shape[0]
    tiles_per_seq = seq // tm
    col_block = lambda k: pl.BlockSpec((D_MODEL, SB_WIDTH), lambda i: (0, k), pipeline_mode=pl.Buffered(1))
    return pl.pallas_call(
        _qkv_kernel,
        grid=(tokens // tm,),
        in_specs=[
            pl.BlockSpec((tm, D_MODEL), lambda i: (i, 0)),
            col_block(0),
            _resident((SB_WIDTH, D_MODEL)),
            col_block(2),
        ],
        out_specs=[
            pl.BlockSpec((tm, SB_WIDTH), lambda i: (i, 0)),
            pl.BlockSpec((1, SB_WIDTH, tm), lambda i: (i // tiles_per_seq, 0, i % tiles_per_seq)),
            pl.BlockSpec((tm, SB_WIDTH), lambda i: (i, 0)),
        ],
        out_shape=[
            jax.ShapeDtypeStruct((tokens, SB_WIDTH), BF16),
            jax.ShapeDtypeStruct((batch, SB_WIDTH, seq), BF16),
            jax.ShapeDtypeStruct((tokens, SB_WIDTH), BF16),
        ],
        compiler_params=pltpu.CompilerParams(
            dimension_semantics=("arbitrary",), vmem_limit_bytes=VMEM_LIMIT),
        name="qkv_proj",
    )(x2, w_in, wkt, w_in)


def _softplus2(z):
    return jnp.maximum(z, 0.0) + jnp.log(1.0 + jnp.exp2(-jnp.abs(z))) * LOG2E


def _sb_attn_kernel(q_ref, kt_ref, v_ref, tri_ref, o_ref, kbd_ref, vbd_ref, pre_ref, rs_ref,
                    carry_ref, acc_ref, *, n_kblocks, sub, npair):
    qs = pl.program_id(2)
    mq = sub * Q_BLOCK
    pairs = range(npair)

    @pl.when(qs == 0)
    def _():
        row_lo = lax.broadcasted_iota(jnp.int32, (PAIR, Q_BLOCK), 0) < HEAD_DIM
        lane_lo = lax.broadcasted_iota(jnp.int32, (Q_BLOCK, PAIR), 1) < HEAD_DIM
        zero = jnp.zeros((PAIR, Q_BLOCK), BF16)
        for p in pairs:
            for j in range(n_kblocks):
                kt = kt_ref[0, p * PAIR:(p + 1) * PAIR, j * Q_BLOCK:(j + 1) * Q_BLOCK]
                kbd_ref[p, j, :, 0:Q_BLOCK] = jnp.where(row_lo, kt, zero)
                kbd_ref[p, j, :, Q_BLOCK:2 * Q_BLOCK] = jnp.where(row_lo, zero, kt)
                vt = v_ref[0, j * Q_BLOCK:(j + 1) * Q_BLOCK, p * PAIR:(p + 1) * PAIR]
                vbd_ref[p, j, 0:Q_BLOCK, :] = jnp.where(lane_lo, vt, zero)
                vbd_ref[p, j, Q_BLOCK:2 * Q_BLOCK, :] = jnp.where(lane_lo, zero, vt)
            vbd_ref[p, n_kblocks] = jnp.zeros((2 * Q_BLOCK, PAIR), BF16)

    tri = tri_ref[...]
    row = lax.broadcasted_iota(jnp.int32, (Q_BLOCK, 2 * Q_BLOCK), 0)
    col = lax.broadcasted_iota(jnp.int32, (Q_BLOCK, 2 * Q_BLOCK), 1) & (Q_BLOCK - 1)
    causal = col < row

    def mask_top(x, fill, diag):
        if not diag:
            return x
        top = jnp.where(causal, x[:Q_BLOCK], fill)
        return top if x.shape[0] == Q_BLOCK else jnp.concatenate([top, x[Q_BLOCK:]], axis=0)

    def stage1(p, r0, r1, j, diag):
        q2 = q_ref[0, r0:r1, p * PAIR:(p + 1) * PAIR]
        z = _dot(q2, kbd_ref[p, j])
        sp = mask_top(_softplus2(z), 0.0, diag)
        hi = sp.astype(BF16)
        lo = (sp - hi.astype(F32)).astype(BF16)
        cr = [_dot(jnp.concatenate([hi[:, h * Q_BLOCK:(h + 1) * Q_BLOCK],
                                    lo[:, h * Q_BLOCK:(h + 1) * Q_BLOCK]], axis=1), tri)
              for h in range(2)]
        pre = z + jnp.concatenate([cr[0][:, :Q_BLOCK], cr[1][:, :Q_BLOCK]], axis=1)
        rs = jnp.concatenate([cr[0][:, Q_BLOCK:], cr[1][:, Q_BLOCK:]], axis=1)
        return pre, rs

    def stage2(p, pre, rs, r0, r1, j, diag, jv=None):
        c = carry_ref[p, r0:r1, :]
        c_new = c + rs
        carry_ref[p, r0:r1, :] = c_new
        w = jnp.exp2(mask_top(pre + c, NEG_FILL, diag)).astype(BF16)
        acc_ref[p, r0:r1, :] += _dot(w, vbd_ref[p, j if jv is None else jv])
        return jnp.max(c_new[-Q_BLOCK:])

    def block(p, r0, r1, j, diag, jv=None):
        pre, rs = stage1(p, r0, r1, j, diag)
        return stage2(p, pre, rs, r0, r1, j, diag, jv)

    def stick_left():
        return functools.reduce(jnp.maximum, [jnp.max(carry_ref[p]) for p in pairs])

    carry_ref[...] = jnp.zeros_like(carry_ref)
    acc_ref[...] = jnp.zeros_like(acc_ref)
    j0 = qs * sub
    j1 = jnp.maximum(j0 - 1, 0)
    jv = jnp.where(qs > 0, j1, n_kblocks)

    stick = []
    for d in reversed(range(-1, sub)):
        r0, r1 = max(d, 0) * Q_BLOCK, min(d + 2, sub) * Q_BLOCK
        for p in pairs:
            left = block(p, r0, r1, j0 + d, True) if d >= 0 else block(p, r0, r1, j1, False, jv)
            if d < sub - 1:
                stick.append(left)

    @pl.when(functools.reduce(jnp.maximum, stick) >= -STICK_CUTOFF)
    def _():
        for d in reversed(range(0, sub - 2)):
            for p in pairs:
                block(p, (d + 2) * Q_BLOCK, mq, j0 + d, False)

        @pl.when(qs > 0)
        def _():
            for p in pairs:
                block(p, Q_BLOCK, mq, j0 - 1, False)

            @pl.when(jnp.logical_and(j0 >= 2, stick_left() >= -STICK_CUTOFF))
            def _():
                for p in pairs:
                    pre, rs = stage1(p, 0, mq, j0 - 2, False)
                    pre_ref[p] = pre
                    rs_ref[p] = rs

                def cond(state):
                    j, alive = state
                    return jnp.logical_and(j >= 1, alive > 0)

                def body(state):
                    j, _ = state
                    for p in pairs:
                        stage2(p, pre_ref[p], rs_ref[p], 0, mq, j, False)
                        pre_n, rs_n = stage1(p, 0, mq, j - 1, False)
                        pre_ref[p] = pre_n
                        rs_ref[p] = rs_n
                    return j - 1, (stick_left() >= -STICK_CUTOFF).astype(jnp.int32)

                j_end, alive = lax.while_loop(cond, body, (j0 - 2, jnp.int32(1)))

                @pl.when(jnp.logical_and(j_end == 0, alive > 0))
                def _():
                    for p in pairs:
                        stage2(p, pre_ref[p], rs_ref[p], 0, mq, 0, False)

    for p in pairs:
        o_ref[0, :, p * PAIR:(p + 1) * PAIR] = acc_ref[p].astype(BF16)


def _sb_attention(q3, kt3, v3, tri, sub, npair):
    batch, seq, _ = q3.shape
    n_blocks = seq // Q_BLOCK
    mq = sub * Q_BLOCK
    width = npair * PAIR
    return pl.pallas_call(
        functools.partial(_sb_attn_kernel, n_kblocks=n_blocks, sub=sub, npair=npair),
        grid=(batch, N_PAIRS // npair, seq // mq),
        in_specs=[
            pl.BlockSpec((1, mq, width), lambda b, g, i: (b, i, g)),
            pl.BlockSpec((1, width, seq), lambda b, g, i: (b, g, 0)),
            pl.BlockSpec((1, seq, width), lambda b, g, i: (b, 0, g)),
            _resident((2 * Q_BLOCK, 2 * Q_BLOCK)),
        ],
        out_specs=pl.BlockSpec((1, mq, width), lambda b, g, i: (b, i, g)),
        out_shape=jax.ShapeDtypeStruct((batch, seq, SB_WIDTH), BF16),
        scratch_shapes=[
            pltpu.VMEM((npair, n_blocks, PAIR, 2 * Q_BLOCK), BF16),
            pltpu.VMEM((npair, n_blocks + 1, 2 * Q_BLOCK, PAIR), BF16),
            pltpu.VMEM((npair, mq, 2 * Q_BLOCK), F32),
            pltpu.VMEM((npair, mq, 2 * Q_BLOCK), F32),
            pltpu.VMEM((npair, mq, 2 * Q_BLOCK), F32),
            pltpu.VMEM((npair, mq, PAIR), F32),
        ],
        compiler_params=pltpu.CompilerParams(
            dimension_semantics=("arbitrary", "arbitrary", "arbitrary"),
            vmem_limit_bytes=VMEM_LIMIT),
        name="sb_attn",
    )(q3, kt3, v3, tri)


def _mixer_out_kernel(x_ref, ya_ref, wu_ref, wvs_ref, wg0_ref, wg1_ref, wg2_ref, wg3_ref, bg_ref, lsg_ref,
                      lsb_ref, wsp_ref, bsp_ref, wa_ref, wb_ref, wo_ref, bo_ref, g1_ref, b1_ref,
                      wup_ref, wdn_ref, o_ref, wup_bf_ref, wdn_bf_ref):
    wup_bf_ref[...] = wup_ref[...].astype(BF16)
    wdn_bf_ref[...] = wdn_ref[...].astype(BF16)
    x = x_ref[...]
    xb = x.astype(BF16)
    tm = x.shape[0]

    u = _gelu_tanh(_dot(xb, wu_ref[...]))
    vs = _ln(_gelu_tanh(_dot(xb, wvs_ref[...])), lsg_ref[...], lsb_ref[...]).astype(BF16)

    tril = (lax.broadcasted_iota(jnp.int32, (CHUNK, CHUNK), 1)
            <= lax.broadcasted_iota(jnp.int32, (CHUNK, CHUNK), 0))
    wsp = [jnp.where(tril, wsp_ref[g], 0.0).astype(BF16) for g in range(N_SG_GROUPS)]
    lane_lo = lax.broadcasted_iota(jnp.int32, (CHUNK, LANES), 1) < HEAD_DIM
    zero = jnp.zeros((CHUNK, LANES), BF16)
    rows = []
    for c in range(tm // CHUNK):
        parts = []
        for p in range(N_SG_GROUPS // 2):
            v2 = vs[c * CHUNK:(c + 1) * CHUNK, p * LANES:(p + 1) * LANES]
            parts.append(_dot(wsp[2 * p], jnp.where(lane_lo, v2, zero))
                         + _dot(wsp[2 * p + 1], jnp.where(lane_lo, zero, v2)))
        rows.append(jnp.concatenate(parts, axis=1) + bsp_ref[...])
    yb = (u * jnp.concatenate(rows, axis=0)).astype(BF16)

    ga = _sigmoid(jnp.concatenate([_dot(xb, wg0_ref[...]), _dot(xb, wg1_ref[...])], axis=1) + bg_ref[:, :D_MODEL])
    gb = _sigmoid(jnp.concatenate([_dot(xb, wg2_ref[...]), _dot(xb, wg3_ref[...])], axis=1) + bg_ref[:, D_MODEL:])
    merged = ga * _dot(ya_ref[...], wa_ref[...]) + gb * _dot(yb, wb_ref[...])
    y = _dot(merged.astype(BF16), wo_ref[...]) + bo_ref[...]
    o_ref[...] = _ln(DN_ALPHA * x + y, g1_ref[...], b1_ref[...])


def _mixer_out(x2, ya2, w_in, bg, lsg, lsb, wsp, bsp, wa, wb, wo, bo, g1, b1, w_up, w_down, tm):
    tokens = x2.shape[0]
    steps = tokens // tm
    col_block = lambda k: pl.BlockSpec((D_MODEL, SB_WIDTH), lambda i: (0, k), pipeline_mode=pl.Buffered(1))
    up_rows = pl.BlockSpec((D_MODEL // steps, 2 * D_FF), lambda i: (i, 0))
    dn_rows = pl.BlockSpec((D_FF // steps, D_MODEL), lambda i: (i, 0))
    return pl.pallas_call(
        _mixer_out_kernel,
        grid=(tokens // tm,),
        in_specs=[
            pl.BlockSpec((tm, D_MODEL), lambda i: (i, 0)),
            pl.BlockSpec((tm, SB_WIDTH), lambda i: (i, 0)),
            col_block(COL_U // SB_WIDTH), col_block(COL_VSG // SB_WIDTH),
            *[col_block(COL_GATE // SB_WIDTH + k) for k in range(2 * D_MODEL // SB_WIDTH)],
            _resident(bg.shape),
            _resident(lsg.shape), _resident(lsb.shape), _resident(wsp.shape), _resident(bsp.shape),
            _resident(wa.shape), _resident(wb.shape), _resident(wo.shape), _resident(bo.shape),
            _resident(g1.shape), _resident(b1.shape),
            up_rows, dn_rows,
        ],
        out_specs=[pl.BlockSpec((tm, D_MODEL), lambda i: (i, 0)), up_rows, dn_rows],
        out_shape=[jax.ShapeDtypeStruct((tokens, D_MODEL), F32),
                   jax.ShapeDtypeStruct(w_up.shape, BF16), jax.ShapeDtypeStruct(w_down.shape, BF16)],
        compiler_params=pltpu.CompilerParams(
            dimension_semantics=("arbitrary",), vmem_limit_bytes=VMEM_LIMIT),
        name="mixer_out",
    )(x2, ya2, w_in, w_in, w_in, w_in, w_in, w_in, bg, lsg, lsb, wsp, bsp, wa, wb, wo, bo, g1, b1,
      w_up, w_down)


def _conv_ffn_kernel(h_ref, wup_ref, cw_ref, cb_ref, wd_ref, bd_ref, g2_ref, b2_ref, o_ref,
                     slab_ref, tail_ref, acc_ref, *, tiles_per_seq):
    tm = h_ref.shape[0] // FF_ROW_SPLIT
    half = FF_SLABS // 2
    PRE = SUBLANES
    seq_start = (pl.program_id(0) % tiles_per_seq) == 0

    def produce(t, hb, c, slot):
        for part in range(2):
            c0 = part * D_FF + c * FF_PAIR
            up = _dot(hb, wup_ref[:, c0:c0 + FF_PAIR])
            for s in range(half):
                slab_ref[t, slot, part * half + s, PRE:tm + PRE, :] = up[:, s * LANES:(s + 1) * LANES]

    def consume(t, c, slot):
        r0 = t * tm
        conv = []
        for s in range(FF_SLABS):
            c0 = (s // half) * D_FF + c * FF_PAIR + (s % half) * LANES
            cw = cw_ref[:, c0:c0 + LANES]
            prev = tail_ref[c, s]
            slab_ref[t, slot, s, 0:PRE, :] = jnp.where(seq_start, 0.0, prev) if t == 0 else prev
            tail_ref[c, s] = slab_ref[t, slot, s, tm:tm + PRE, :]
            conv.append(cb_ref[:, c0:c0 + LANES]
                        + cw[0:1] * slab_ref[t, slot, s, PRE - 2:tm + PRE - 2, :]
                        + cw[1:2] * slab_ref[t, slot, s, PRE - 1:tm + PRE - 1, :]
                        + cw[2:3] * slab_ref[t, slot, s, PRE:tm + PRE, :])
        act = jnp.concatenate([(conv[s] * _sigmoid(conv[s]) * conv[half + s]).astype(BF16)
                               for s in range(half)], axis=1)
        acc_ref[r0:r0 + tm, :] += _dot(act, wd_ref[c * FF_PAIR:(c + 1) * FF_PAIR, :])

    acc_ref[...] = DN_ALPHA * h_ref[...] + bd_ref[...]
    for t in range(FF_ROW_SPLIT):
        r0 = t * tm
        hb = h_ref[r0:r0 + tm, :].astype(BF16)
        produce(t, hb, 0, 0)
        for c in range(FF_STEPS):
            if c + 1 < FF_STEPS:
                produce(t, hb, c + 1, (c + 1) % 2)
            consume(t, c, c % 2)
        o_ref[r0:r0 + tm, :] = _ln(acc_ref[r0:r0 + tm, :], g2_ref[...], b2_ref[...])


def _conv_ffn(h2, wup, cw, cb, wd, bd, g2, b2, seq, tm):
    tokens = h2.shape[0]
    return pl.pallas_call(
        functools.partial(_conv_ffn_kernel, tiles_per_seq=seq // tm),
        grid=(tokens // tm,),
        in_specs=[
            pl.BlockSpec((tm, D_MODEL), lambda i: (i, 0)),
            _resident(wup.shape), _resident(cw.shape), _resident(cb.shape), _resident(wd.shape),
            _resident(bd.shape), _resident(g2.shape), _resident(b2.shape),
        ],
        out_specs=pl.BlockSpec((tm, D_MODEL), lambda i: (i, 0)),
        out_shape=jax.ShapeDtypeStruct((tokens, D_MODEL), F32),
        scratch_shapes=[
            pltpu.VMEM((FF_ROW_SPLIT, 2, FF_SLABS, tm // FF_ROW_SPLIT + SUBLANES, LANES), F32),
            pltpu.VMEM((FF_STEPS, FF_SLABS, SUBLANES, LANES), F32),
            pltpu.VMEM((tm, D_MODEL), F32),
        ],
        compiler_params=pltpu.CompilerParams(
            dimension_semantics=("arbitrary",), vmem_limit_bytes=VMEM_LIMIT),
        name="conv_ffn",
    )(h2, wup, cw, cb, wd, bd, g2, b2)


def _tri_const():
    j = jnp.arange(Q_BLOCK)[:, None]
    s = jnp.arange(Q_BLOCK)[None, :]
    half = jnp.concatenate([-(j >= s).astype(F32), -jnp.ones((Q_BLOCK, Q_BLOCK), F32)], axis=1)
    return jnp.concatenate([half, half], axis=0).astype(BF16)


def _layer(h, w_in, b_gate, ln_sg_g, ln_sg_b, w_spatial, b_spatial, w_branch_a, w_branch_b, w_out,
           b_out, ln1_g, ln1_b, w_up, conv_w, conv_b, w_down, b_down, ln2_g, ln2_b):
    batch, seq, _ = h.shape
    tokens = batch * seq
    x2 = h.reshape(tokens, D_MODEL)
    row = lambda v: v.reshape(1, -1)

    wkt = w_in[:, SB_WIDTH:2 * SB_WIDTH].T
    bsp = jnp.repeat(b_spatial.T, SG_WIDTH // N_SG_GROUPS, axis=1)

    q2, kt3, v2 = _qkv_proj(x2, w_in, wkt, batch, seq, tm=QKV_TM)
    ya3 = _sb_attention(q2.reshape(batch, seq, SB_WIDTH), kt3, v2.reshape(batch, seq, SB_WIDTH),
                        _tri_const(), sub=ATTN_SUB, npair=ATTN_NPAIR)
    h1, w_up_bf, w_down_bf = _mixer_out(x2, ya3.reshape(tokens, SB_WIDTH), w_in, row(b_gate), row(ln_sg_g),
                                        row(ln_sg_b), w_spatial, bsp, w_branch_a, w_branch_b, w_out,
                                        row(b_out), row(ln1_g), row(ln1_b), w_up, w_down, tm=MIXER_TM)
    h2 = _conv_ffn(h1, w_up_bf, conv_w, row(conv_b), w_down_bf, row(b_down), row(ln2_g), row(ln2_b), seq,
                   tm=FFN_TM)
    return h2.reshape(batch, seq, D_MODEL)


def kernel(x, w_in, b_gate, ln_sg_g, ln_sg_b, w_spatial, b_spatial, w_branch_a, w_branch_b, w_out,
           b_out, ln1_g, ln1_b, w_up, conv_w, conv_b, w_down, b_down, ln2_g, ln2_b):
    h = x
    for l in range(w_in.shape[0]):
        h = _layer(h, w_in[l], b_gate[l], ln_sg_g[l], ln_sg_b[l], w_spatial[l], b_spatial[l],
                   w_branch_a[l], w_branch_b[l], w_out[l], b_out[l], ln1_g[l], ln1_b[l], w_up[l],
                   conv_w[l], conv_b[l], w_down[l], b_down[l], ln2_g[l], ln2_b[l])
    return h
```

```python
import functools

import jax
import jax.numpy as jnp
from jax import lax
from jax.experimental import pallas as pl
from jax.experimental.pallas import tpu as pltpu

F32 = jnp.float32
BF16 = jnp.bfloat16

D_MODEL = 1024
N_HEADS = 8
HEAD_DIM = 64
SB_WIDTH = N_HEADS * HEAD_DIM
SG_WIDTH = 512
COL_U = 3 * SB_WIDTH
COL_VSG = COL_U + SG_WIDTH
COL_GATE = COL_VSG + SG_WIDTH
N_SG_GROUPS = 8
CHUNK = 128
Q_BLOCK = 128
D_FF = 2816
LN_EPS = 1e-5
LOG2E = 1.4426950408889634
SCORE_SCALE = HEAD_DIM ** -0.5 * LOG2E
NEG_FILL = -1e30
STICK_CUTOFF = 64.0 * LOG2E
DN_ALPHA = 2.0 ** 0.25

LANES = 128
SUBLANES = 8
PAIR = 2 * HEAD_DIM
N_PAIRS = N_HEADS // 2
FF_PAIR = 256
FF_STEPS = D_FF // FF_PAIR
FF_SLABS = 2 * FF_PAIR // LANES

QKV_TM = 2048
ATTN_SUB = 8
ATTN_NPAIR = 2
MIXER_TM = 1024
FFN_TM = 1024
FF_ROW_SPLIT = 4
VMEM_LIMIT = 56 * 1024 * 1024


def _ln(x, g, b):
    mu = jnp.mean(x, axis=-1, keepdims=True)
    xc = x - mu
    var = jnp.mean(xc * xc, axis=-1, keepdims=True)
    return xc * lax.rsqrt(var + LN_EPS) * g + b


def _gelu_tanh(x):
    c = 0.7978845608028654
    return x * (0.5 * (1.0 + jnp.tanh(c * (x + 0.044715 * (x * x * x)))))


def _sigmoid(x):
    return 1.0 / (1.0 + jnp.exp(-x))


def _dot(a, b):
    return lax.dot_general(a, b, (((1,), (0,)), ((), ())), preferred_element_type=F32)


def _resident(shape):
    nd = len(shape)
    return pl.BlockSpec(shape, lambda *_: (0,) * nd, pipeline_mode=pl.Buffered(1))


def _qkv_kernel(x_ref, wq_ref, wk_ref, wv_ref, q_ref, k_ref, v_ref):
    x = x_ref[...].astype(BF16)
    q_ref[...] = (_dot(x, wq_ref[...]) * SCORE_SCALE).astype(BF16)
    k_ref[...] = _dot(x, wk_ref[...]).astype(BF16)
    v_ref[...] = _dot(x, wv_ref[...]).astype(BF16)


def _qkv_proj(x2, w_in, tm):
    tokens = x2.shape[0]
    col_block = lambda k: pl.BlockSpec((D_MODEL, SB_WIDTH), lambda i: (0, k), pipeline_mode=pl.Buffered(1))
    return pl.pallas_call(
        _qkv_kernel,
        grid=(tokens // tm,),
        in_specs=[
            pl.BlockSpec((tm, D_MODEL), lambda i: (i, 0)),
            col_block(0), col_block(1), col_block(2),
        ],
        out_specs=[pl.BlockSpec((tm, SB_WIDTH), lambda i: (i, 0))] * 3,
        out_shape=[jax.ShapeDtypeStruct((tokens, SB_WIDTH), BF16)] * 3,
        compiler_params=pltpu.CompilerParams(
            dimension_semantics=("arbitrary",), vmem_limit_bytes=VMEM_LIMIT),
        name="qkv_proj",
    )(x2, w_in, w_in, w_in)


def _softplus2(z):
    return jnp.maximum(z, 0.0) + jnp.log(1.0 + jnp.exp2(-jnp.abs(z))) * LOG2E


def _sb_attn_kernel(q_ref, k_ref, v_ref, tri_ref, o_ref, kbd_ref, vbd_ref, pre_ref, rs_ref,
                    carry_ref, acc_ref, *, n_kblocks, sub, npair):
    qs = pl.program_id(2)
    mq = sub * Q_BLOCK
    pairs = range(npair)

    @pl.when(qs == 0)
    def _():
        row_lo = lax.broadcasted_iota(jnp.int32, (PAIR, Q_BLOCK), 0) < HEAD_DIM
        lane_lo = lax.broadcasted_iota(jnp.int32, (Q_BLOCK, PAIR), 1) < HEAD_DIM
        zero = jnp.zeros((PAIR, Q_BLOCK), BF16)
        for p in pairs:
            for j in range(n_kblocks):
                kt = k_ref[0, j * Q_BLOCK:(j + 1) * Q_BLOCK, p * PAIR:(p + 1) * PAIR].T
                kbd_ref[p, j, :, 0:Q_BLOCK] = jnp.where(row_lo, kt, zero)
                kbd_ref[p, j, :, Q_BLOCK:2 * Q_BLOCK] = jnp.where(row_lo, zero, kt)
                vt = v_ref[0, j * Q_BLOCK:(j + 1) * Q_BLOCK, p * PAIR:(p + 1) * PAIR]
                vbd_ref[p, j, 0:Q_BLOCK, :] = jnp.where(lane_lo, vt, zero)
                vbd_ref[p, j, Q_BLOCK:2 * Q_BLOCK, :] = jnp.where(lane_lo, zero, vt)
            vbd_ref[p, n_kblocks] = jnp.zeros((2 * Q_BLOCK, PAIR), BF16)

    tri = tri_ref[...]
    row = lax.broadcasted_iota(jnp.int32, (Q_BLOCK, 2 * Q_BLOCK), 0)
    col = lax.broadcasted_iota(jnp.int32, (Q_BLOCK, 2 * Q_BLOCK), 1) & (Q_BLOCK - 1)
    causal = col < row

    def mask_top(x, fill, diag):
        if not diag:
            return x
        top = jnp.where(causal, x[:Q_BLOCK], fill)
        return top if x.shape[0] == Q_BLOCK else jnp.concatenate([top, x[Q_BLOCK:]], axis=0)

    def stage1(p, r0, r1, j, diag):
        q2 = q_ref[0, r0:r1, p * PAIR:(p + 1) * PAIR]
        z = _dot(q2, kbd_ref[p, j])
        sp = mask_top(_softplus2(z), 0.0, diag)
        hi = sp.astype(BF16)
        lo = (sp - hi.astype(F32)).astype(BF16)
        cr = [_dot(jnp.concatenate([hi[:, h * Q_BLOCK:(h + 1) * Q_BLOCK],
                                    lo[:, h * Q_BLOCK:(h + 1) * Q_BLOCK]], axis=1), tri)
              for h in range(2)]
        pre = z + jnp.concatenate([cr[0][:, :Q_BLOCK], cr[1][:, :Q_BLOCK]], axis=1)
        rs = jnp.concatenate([cr[0][:, Q_BLOCK:], cr[1][:, Q_BLOCK:]], axis=1)
        return pre, rs

    def stage2(p, pre, rs, r0, r1, j, diag, jv=None):
        c = carry_ref[p, r0:r1, :]
        c_new = c + rs
        carry_ref[p, r0:r1, :] = c_new
        w = jnp.exp2(mask_top(pre + c, NEG_FILL, diag)).astype(BF16)
        acc_ref[p, r0:r1, :] += _dot(w, vbd_ref[p, j if jv is None else jv])
        return jnp.max(c_new[-Q_BLOCK:])

    def block(p, r0, r1, j, diag, jv=None):
        pre, rs = stage1(p, r0, r1, j, diag)
        return stage2(p, pre, rs, r0, r1, j, diag, jv)

    def stick_left():
        return functools.reduce(jnp.maximum, [jnp.max(carry_ref[p]) for p in pairs])

    carry_ref[...] = jnp.zeros_like(carry_ref)
    acc_ref[...] = jnp.zeros_like(acc_ref)
    j0 = qs * sub
    j1 = jnp.maximum(j0 - 1, 0)
    jv = jnp.where(qs > 0, j1, n_kblocks)

    stick = []
    for d in reversed(range(-1, sub)):
        r0, r1 = max(d, 0) * Q_BLOCK, min(d + 2, sub) * Q_BLOCK
        for p in pairs:
            left = block(p, r0, r1, j0 + d, True) if d >= 0 else block(p, r0, r1, j1, False, jv)
            if d < sub - 1:
                stick.append(left)

    @pl.when(functools.reduce(jnp.maximum, stick) >= -STICK_CUTOFF)
    def _():
        for d in reversed(range(0, sub - 2)):
            for p in pairs:
                block(p, (d + 2) * Q_BLOCK, mq, j0 + d, False)

        @pl.when(qs > 0)
        def _():
            for p in pairs:
                block(p, Q_BLOCK, mq, j0 - 1, False)

            @pl.when(jnp.logical_and(j0 >= 2, stick_left() >= -STICK_CUTOFF))
            def _():
                for p in pairs:
                    pre, rs = stage1(p, 0, mq, j0 - 2, False)
                    pre_ref[p] = pre
                    rs_ref[p] = rs

                def cond(state):
                    j, alive = state
                    return jnp.logical_and(j >= 1, alive > 0)

                def body(state):
                    j, _ = state
                    for p in pairs:
                        stage2(p, pre_ref[p], rs_ref[p], 0, mq, j, False)
                        pre_n, rs_n = stage1(p, 0, mq, j - 1, False)
                        pre_ref[p] = pre_n
                        rs_ref[p] = rs_n
                    return j - 1, (stick_left() >= -STICK_CUTOFF).astype(jnp.int32)

                j_end, alive = lax.while_loop(cond, body, (j0 - 2, jnp.int32(1)))

                @pl.when(jnp.logical_and(j_end == 0, alive > 0))
                def _():
                    for p in pairs:
                        stage2(p, pre_ref[p], rs_ref[p], 0, mq, 0, False)

    for p in pairs:
        o_ref[0, :, p * PAIR:(p + 1) * PAIR] = acc_ref[p].astype(BF16)


def _sb_attention(q3, kt3, v3, tri, sub, npair):
    batch, seq, _ = q3.shape
    n_blocks = seq // Q_BLOCK
    mq = sub * Q_BLOCK
    width = npair * PAIR
    return pl.pallas_call(
        functools.partial(_sb_attn_kernel, n_kblocks=n_blocks, sub=sub, npair=npair),
        grid=(batch, N_PAIRS // npair, seq // mq),
        in_specs=[
            pl.BlockSpec((1, mq, width), lambda b, g, i: (b, i, g)),
            pl.BlockSpec((1, seq, width), lambda b, g, i: (b, 0, g)),
            pl.BlockSpec((1, seq, width), lambda b, g, i: (b, 0, g)),
            _resident((2 * Q_BLOCK, 2 * Q_BLOCK)),
        ],
        out_specs=pl.BlockSpec((1, mq, width), lambda b, g, i: (b, i, g)),
        out_shape=jax.ShapeDtypeStruct((batch, seq, SB_WIDTH), BF16),
        scratch_shapes=[
            pltpu.VMEM((npair, n_blocks, PAIR, 2 * Q_BLOCK), BF16),
            pltpu.VMEM((npair, n_blocks + 1, 2 * Q_BLOCK, PAIR), BF16),
            pltpu.VMEM((npair, mq, 2 * Q_BLOCK), F32),
            pltpu.VMEM((npair, mq, 2 * Q_BLOCK), F32),
            pltpu.VMEM((npair, mq, 2 * Q_BLOCK), F32),
            pltpu.VMEM((npair, mq, PAIR), F32),
        ],
        compiler_params=pltpu.CompilerParams(
            dimension_semantics=("arbitrary", "arbitrary", "arbitrary"),
            vmem_limit_bytes=VMEM_LIMIT),
        name="sb_attn",
    )(q3, kt3, v3, tri)


def _mixer_out_kernel(x_ref, ya_ref, wu_ref, wvs_ref, wg0_ref, wg1_ref, wg2_ref, wg3_ref, bg_ref, lsg_ref,
                      lsb_ref, wsp_ref, bsp_ref, wa_ref, wb_ref, wo_ref, bo_ref, g1_ref, b1_ref,
                      wup_ref, wdn_ref, o_ref, wup_bf_ref, wdn_bf_ref):
    wup_bf_ref[...] = wup_ref[...].astype(BF16)
    wdn_bf_ref[...] = wdn_ref[...].astype(BF16)
    x = x_ref[...]
    xb = x.astype(BF16)
    tm = x.shape[0]

    u = _gelu_tanh(_dot(xb, wu_ref[...]))
    vs = _ln(_gelu_tanh(_dot(xb, wvs_ref[...])), lsg_ref[...], lsb_ref[...]).astype(BF16)

    tril = (lax.broadcasted_iota(jnp.int32, (CHUNK, CHUNK), 1)
            <= lax.broadcasted_iota(jnp.int32, (CHUNK, CHUNK), 0))
    wsp = [jnp.where(tril, wsp_ref[g], 0.0).astype(BF16) for g in range(N_SG_GROUPS)]
    lane_lo = lax.broadcasted_iota(jnp.int32, (CHUNK, LANES), 1) < HEAD_DIM
    zero = jnp.zeros((CHUNK, LANES), BF16)
    rows = []
    for c in range(tm // CHUNK):
        parts = []
        for p in range(N_SG_GROUPS // 2):
            v2 = vs[c * CHUNK:(c + 1) * CHUNK, p * LANES:(p + 1) * LANES]
            parts.append(_dot(wsp[2 * p], jnp.where(lane_lo, v2, zero))
                         + _dot(wsp[2 * p + 1], jnp.where(lane_lo, zero, v2)))
        rows.append(jnp.concatenate(parts, axis=1) + bsp_ref[...])
    yb = (u * jnp.concatenate(rows, axis=0)).astype(BF16)

    ga = _sigmoid(jnp.concatenate([_dot(xb, wg0_ref[...]), _dot(xb, wg1_ref[...])], axis=1) + bg_ref[:, :D_MODEL])
    gb = _sigmoid(jnp.concatenate([_dot(xb, wg2_ref[...]), _dot(xb, wg3_ref[...])], axis=1) + bg_ref[:, D_MODEL:])
    merged = ga * _dot(ya_ref[...], wa_ref[...]) + gb * _dot(yb, wb_ref[...])
    y = _dot(merged.astype(BF16), wo_ref[...]) + bo_ref[...]
    o_ref[...] = _ln(DN_ALPHA * x + y, g1_ref[...], b1_ref[...])


def _mixer_out(x2, ya2, w_in, bg, lsg, lsb, wsp, bsp, wa, wb, wo, bo, g1, b1, w_up, w_down, tm):
    tokens = x2.shape[0]
    steps = tokens // tm
    col_block = lambda k: pl.BlockSpec((D_MODEL, SB_WIDTH), lambda i: (0, k), pipeline_mode=pl.Buffered(1))
    up_rows = pl.BlockSpec((D_MODEL // steps, 2 * D_FF), lambda i: (i, 0))
    dn_rows = pl.BlockSpec((D_FF // steps, D_MODEL), lambda i: (i, 0))
    return pl.pallas_call(
        _mixer_out_kernel,
        grid=(tokens // tm,),
        in_specs=[
            pl.BlockSpec((tm, D_MODEL), lambda i: (i, 0)),
            pl.BlockSpec((tm, SB_WIDTH), lambda i: (i, 0)),
            col_block(COL_U // SB_WIDTH), col_block(COL_VSG // SB_WIDTH),
            *[col_block(COL_GATE // SB_WIDTH + k) for k in range(2 * D_MODEL // SB_WIDTH)],
            _resident(bg.shape),
            _resident(lsg.shape), _resident(lsb.shape), _resident(wsp.shape), _resident(bsp.shape),
            _resident(wa.shape), _resident(wb.shape), _resident(wo.shape), _resident(bo.shape),
            _resident(g1.shape), _resident(b1.shape),
            up_rows, dn_rows,
        ],
        out_specs=[pl.BlockSpec((tm, D_MODEL), lambda i: (i, 0)), up_rows, dn_rows],
        out_shape=[jax.ShapeDtypeStruct((tokens, D_MODEL), F32),
                   jax.ShapeDtypeStruct(w_up.shape, BF16), jax.ShapeDtypeStruct(w_down.shape, BF16)],
        compiler_params=pltpu.CompilerParams(
            dimension_semantics=("arbitrary",), vmem_limit_bytes=VMEM_LIMIT),
        name="mixer_out",
    )(x2, ya2, w_in, w_in, w_in, w_in, w_in, w_in, bg, lsg, lsb, wsp, bsp, wa, wb, wo, bo, g1, b1,
      w_up, w_down)


def _conv_ffn_kernel(h_ref, wup_ref, cw_ref, cb_ref, wd_ref, bd_ref, g2_ref, b2_ref, o_ref,
                     slab_ref, tail_ref, acc_ref, *, tiles_per_seq):
    tm = h_ref.shape[0] // FF_ROW_SPLIT
    half = FF_SLABS // 2
    PRE = SUBLANES
    seq_start = (pl.program_id(0) % tiles_per_seq) == 0

    def produce(t, hb, c, slot):
        for part in range(2):
            c0 = part * D_FF + c * FF_PAIR
            up = _dot(hb, wup_ref[:, c0:c0 + FF_PAIR])
            for s in range(half):
                slab_ref[t, slot, part * half + s, PRE:tm + PRE, :] = up[:, s * LANES:(s + 1) * LANES]

    def consume(t, c, slot):
        r0 = t * tm
        conv = []
        for s in range(FF_SLABS):
            c0 = (s // half) * D_FF + c * FF_PAIR + (s % half) * LANES
            cw = cw_ref[:, c0:c0 + LANES]
            prev = tail_ref[c, s]
            slab_ref[t, slot, s, 0:PRE, :] = jnp.where(seq_start, 0.0, prev) if t == 0 else prev
            tail_ref[c, s] = slab_ref[t, slot, s, tm:tm + PRE, :]
            conv.append(cb_ref[:, c0:c0 + LANES]
                        + cw[0:1] * slab_ref[t, slot, s, PRE - 2:tm + PRE - 2, :]
                        + cw[1:2] * slab_ref[t, slot, s, PRE - 1:tm + PRE - 1, :]
                        + cw[2:3] * slab_ref[t, slot, s, PRE:tm + PRE, :])
        act = jnp.concatenate([(conv[s] * _sigmoid(conv[s]) * conv[half + s]).astype(BF16)
                               for s in range(half)], axis=1)
        acc_ref[r0:r0 + tm, :] += _dot(act, wd_ref[c * FF_PAIR:(c + 1) * FF_PAIR, :])

    acc_ref[...] = DN_ALPHA * h_ref[...] + bd_ref[...]
    for t in range(FF_ROW_SPLIT):
        r0 = t * tm
        hb = h_ref[r0:r0 + tm, :].astype(BF16)
        produce(t, hb, 0, 0)
        for c in range(FF_STEPS):
            if c + 1 < FF_STEPS:
                produce(t, hb, c + 1, (c + 1) % 2)
            consume(t, c, c % 2)
        o_ref[r0:r0 + tm, :] = _ln(acc_ref[r0:r0 + tm, :], g2_ref[...], b2_ref[...])


def _conv_ffn(h2, wup, cw, cb, wd, bd, g2, b2, seq, tm):
    tokens = h2.shape[0]
    return pl.pallas_call(
        functools.partial(_conv_ffn_kernel, tiles_per_seq=seq // tm),
        grid=(tokens // tm,),
        in_specs=[
            pl.BlockSpec((tm, D_MODEL), lambda i: (i, 0)),
            _resident(wup.shape), _resident(cw.shape), _resident(cb.shape), _resident(wd.shape),
            _resident(bd.shape), _resident(g2.shape), _resident(b2.shape),
        ],
        out_specs=pl.BlockSpec((tm, D_MODEL), lambda i: (i, 0)),
        out_shape=jax.ShapeDtypeStruct((tokens, D_MODEL), F32),
        scratch_shapes=[
            pltpu.VMEM((FF_ROW_SPLIT, 2, FF_SLABS, tm // FF_ROW_SPLIT + SUBLANES, LANES), F32),
            pltpu.VMEM((FF_STEPS, FF_SLABS, SUBLANES, LANES), F32),
            pltpu.VMEM((tm, D_MODEL), F32),
        ],
        compiler_params=pltpu.CompilerParams(
            dimension_semantics=("arbitrary",), vmem_limit_bytes=VMEM_LIMIT),
        name="conv_ffn",
    )(h2, wup, cw, cb, wd, bd, g2, b2)


def _tri_const():
    j = jnp.arange(Q_BLOCK)[:, None]
    s = jnp.arange(Q_BLOCK)[None, :]
    half = jnp.concatenate([-(j >= s).astype(F32), -jnp.ones((Q_BLOCK, Q_BLOCK), F32)], axis=1)
    return jnp.concatenate([half, half], axis=0).astype(BF16)


def _layer(h, w_in, b_gate, ln_sg_g, ln_sg_b, w_spatial, b_spatial, w_branch_a, w_branch_b, w_out,
           b_out, ln1_g, ln1_b, w_up, conv_w, conv_b, w_down, b_down, ln2_g, ln2_b):
    batch, seq, _ = h.shape
    tokens = batch * seq
    x2 = h.reshape(tokens, D_MODEL)
    row = lambda v: v.reshape(1, -1)

    bsp = jnp.repeat(b_spatial.T, SG_WIDTH // N_SG_GROUPS, axis=1)

    q2, k2, v2 = _qkv_proj(x2, w_in, tm=QKV_TM)
    ya3 = _sb_attention(q2.reshape(batch, seq, SB_WIDTH), k2.reshape(batch, seq, SB_WIDTH), v2.reshape(batch, seq, SB_WIDTH),
                        _tri_const(), sub=ATTN_SUB, npair=ATTN_NPAIR)
    h1, w_up_bf, w_down_bf = _mixer_out(x2, ya3.reshape(tokens, SB_WIDTH), w_in, row(b_gate), row(ln_sg_g),
                                        row(ln_sg_b), w_spatial, bsp, w_branch_a, w_branch_b, w_out,
                                        row(b_out), row(ln1_g), row(ln1_b), w_up, w_down, tm=MIXER_TM)
    h2 = _conv_ffn(h1, w_up_bf, conv_w, row(conv_b), w_down_bf, row(b_down), row(ln2_g), row(ln2_b), seq,
                   tm=FFN_TM)
    return h2.reshape(batch, seq, D_MODEL)


def kernel(x, w_in, b_gate, ln_sg_g, ln_sg_b, w_spatial, b_spatial, w_branch_a, w_branch_b, w_out,
           b_out, ln1_g, ln1_b, w_up, conv_w, conv_b, w_down, b_down, ln2_g, ln2_b):
    h = x
    for l in range(w_in.shape[0]):
        h = _layer(h, w_in[l], b_gate[l], ln_sg_g[l], ln_sg_b[l], w_spatial[l], b_spatial[l],
                   w_branch_a[l], w_branch_b[l], w_out[l], b_out[l], ln1_g[l], ln1_b[l], w_up[l],
                   conv_w[l], conv_b[l], w_down[l], b_down[l], ln2_g[l], ln2_b[l])
    return h
```
